```python
import math
import jax, jax.numpy as jnp
from jax import lax
import numpy as np

D_MODEL = 1024
BATCH = 2
SEQ = 8192
DEPTH = 4
DEC_BATCH = 128
DEC_SEQ = 4
PAST_LEN = 2048
PAGE_SIZE = 128

HEAD_DIM = 64
WINDOWS = (128, 512, 2048)
DILATIONS = (1, 4, 16)
N_ATTN_GROUPS = 3
ATTN_HPG = D_MODEL // 256
ATTN_HEADS = N_ATTN_GROUPS * ATTN_HPG
ATTN_DIM = ATTN_HEADS * HEAD_DIM
ATTN_OUT_DIM = ATTN_HPG * HEAD_DIM
ROPE_THETA = 10000.0

SSM_EXPAND = 2
SSM_INNER = SSM_EXPAND * D_MODEL
SSM_HEAD_DIM = 64
SSM_HEADS = SSM_INNER // SSM_HEAD_DIM
SSM_GROUPS = 4
SSM_HPG = SSM_HEADS // SSM_GROUPS
SSM_STATE = 128
CONV_WIDTH = 4
SSD_CHUNK = 128
XBC_DIM = SSM_INNER + 2 * SSM_GROUPS * SSM_STATE
DT_MIN = 0.001
DT_MAX = 0.1

D_FF = 4 * D_MODEL
PLE_DIM = 256
NORM_EPS = 1e-6

IN_SIZES = (ATTN_DIM, ATTN_DIM, ATTN_DIM, SSM_INNER, XBC_DIM, SSM_HEADS, D_MODEL, D_MODEL)
IN_DIM = sum(IN_SIZES)

kernel_name = 'hybrid_dilated_attn_ssd_decoder_step'


def split_cols(t, sizes):
    out, start = [], 0
    for s in sizes:
        out.append(t[..., start:start + s])
        start += s
    return out


def rms_norm(x, w):
    x32 = x.astype(jnp.float32)
    y = x32 * lax.rsqrt(jnp.mean(x32 * x32, axis=-1, keepdims=True) + NORM_EPS)
    return (y * w.astype(jnp.float32)).astype(x.dtype)


def rope(t, pos):
    half = t.shape[-1] // 2
    inv_freq = ROPE_THETA ** (-jnp.arange(half, dtype=jnp.float32) / half)
    ang = pos.astype(jnp.float32)[:, None] * inv_freq[None, :]
    cos = jnp.cos(ang)[None, :, None, :]
    sin = jnp.sin(ang)[None, :, None, :]
    t32 = t.astype(jnp.float32)
    t1, t2 = t32[..., :half], t32[..., half:]
    return jnp.concatenate([t1 * cos - t2 * sin, t1 * sin + t2 * cos], axis=-1).astype(t.dtype)


def dilated_attn_prompt(q, k, v, window, dilation):
    bsz, S, H, C = q.shape
    n = window // dilation
    blk = n
    span = dilation * blk
    Sp = -(-S // span) * span
    M = Sp // dilation
    nb = M // blk

    def to_blocks(t):
        t = jnp.pad(t, ((0, 0), (0, Sp - S), (0, 0), (0, 0)))
        t = t.reshape(bsz, M, dilation, H, C).transpose(0, 2, 1, 3, 4)
        return t.reshape(bsz, dilation, nb, blk, H, C)

    qb, kb, vb = to_blocks(q), to_blocks(k), to_blocks(v)

    def with_prev(t):
        prev = jnp.pad(t, ((0, 0), (0, 0), (1, 0), (0, 0), (0, 0), (0, 0)))[:, :, :-1]
        return jnp.concatenate([prev, t], axis=3)

    kk, vv = with_prev(kb), with_prev(vb)
    s = jnp.einsum('brnqhc,brnkhc->brnhqk', qb, kk, preferred_element_type=jnp.float32) * (HEAD_DIM ** -0.5)
    qi = jnp.arange(blk)[:, None]
    ki = jnp.arange(2 * blk)[None, :]
    diff = blk + qi - ki
    band = (diff >= 0) & (diff <= n)
    not_before_start = (jnp.arange(nb)[:, None, None] > 0) | (ki[None] >= blk)
    mask = band[None] & not_before_start
    s = jnp.where(mask[None, None, :, None], s, -jnp.inf)
    lse = jax.nn.logsumexp(s, axis=-1)
    pr = jnp.exp(s - lse[..., None])
    o = jnp.einsum('brnhqk,brnkhc->brnqhc', pr, vv.astype(jnp.float32))
    o = o.reshape(bsz, dilation, M, H, C).transpose(0, 2, 1, 3, 4).reshape(bsz, Sp, H, C)[:, :S]
    lse = lse.transpose(0, 1, 2, 4, 3).reshape(bsz, dilation, M, H).transpose(0, 2, 1, 3).reshape(bsz, Sp, H)[:, :S]
    return o, lse


def dilated_attn_sample(q, k_all, v_all, lwin, window, dilation):
    T = q.shape[1]
    n = window // dilation
    idx = lwin + jnp.arange(T)[:, None] - dilation * jnp.arange(n + 1)[None, :]
    valid = idx >= 0
    idx = jnp.maximum(idx, 0)
    kg = k_all[:, idx]
    vg = v_all[:, idx]
    s = jnp.einsum('bthc,btmhc->bthm', q, kg, preferred_element_type=jnp.float32) * (HEAD_DIM ** -0.5)
    s = jnp.where(valid[None, :, None, :], s, -jnp.inf)
    lse = jax.nn.logsumexp(s, axis=-1)
    pr = jnp.exp(s - lse[..., None])
    o = jnp.einsum('bthm,btmhc->bthc', pr, vg.astype(jnp.float32))
    return o, lse


def ssd_scan(xh, dt, a, bm, cm, h0):
    bsz, L = xh.shape[0], xh.shape[1]
    q = min(SSD_CHUNK, L)
    nc = -(-L // q)
    pad = nc * q - L

    def chunked(t):
        t = jnp.pad(t.astype(jnp.float32), [(0, 0), (0, pad)] + [(0, 0)] * (t.ndim - 2))
        return t.reshape((bsz, nc, q) + t.shape[2:])

    xdt = chunked(xh.astype(jnp.float32) * dt[..., None]).reshape(bsz, nc, q, SSM_GROUPS, SSM_HPG, SSM_HEAD_DIM)
    da = chunked(dt * a).reshape(bsz, nc, q, SSM_GROUPS, SSM_HPG)
    bc = chunked(bm)
    cc = chunked(cm)
    acum = jnp.cumsum(da, axis=2)
    causal = jnp.tril(jnp.ones((q, q), dtype=bool))[:, :, None, None]
    seg = acum[:, :, :, None] - acum[:, :, None, :]
    decay = jnp.exp(jnp.where(causal, seg, -jnp.inf))
    cb = jnp.einsum('bclgn,bcsgn->bclsg', cc, bc)
    y_diag = jnp.einsum('bclsg,bclsge,bcsgep->bclgep', cb, decay, xdt)
    decay_end = jnp.exp(acum[:, :, -1:] - acum)
    chunk_states = jnp.einsum('bclgn,bclge,bclgep->bcgepn', bc, decay_end, xdt)
    chunk_decay = jnp.exp(acum[:, :, -1])

    def step(h, inp):
        s_c, d_c = inp
        return h * d_c[..., None, None] + s_c, h

    h0g = h0.astype(jnp.float32).reshape(bsz, SSM_GROUPS, SSM_HPG, SSM_HEAD_DIM, SSM_STATE)
    h_last, h_in = lax.scan(step, h0g, (jnp.moveaxis(chunk_states, 1, 0), jnp.moveaxis(chunk_decay, 1, 0)))
    h_in = jnp.moveaxis(h_in, 0, 1)
    y_off = jnp.einsum('bclgn,bcgepn,bclge->bclgep', cc, h_in, jnp.exp(acum))
    y = (y_diag + y_off).reshape(bsz, nc * q, SSM_HEADS, SSM_HEAD_DIM)[:, :L]
    return y, h_last.reshape(bsz, SSM_HEADS, SSM_HEAD_DIM, SSM_STATE)


def block(x, ple, pos, kv_bufs, conv_buf, ssm_h, lw, is_sample):
    (n1, w_in_i, cw, cbias, dtb, alog, dsk, snw, wao, wso, wo, n2, wup, wdn, wpp, wpg) = lw
    bsz, L, _ = x.shape
    u = rms_norm(x, n1)
    q, k, v, z, xbc, dt_raw, g_a, g_b = split_cols(u @ w_in_i, IN_SIZES)

    q = rope(q.reshape(bsz, L, ATTN_HEADS, HEAD_DIM), pos).reshape(bsz, L, N_ATTN_GROUPS, ATTN_HPG, HEAD_DIM)
    k = rope(k.reshape(bsz, L, ATTN_HEADS, HEAD_DIM), pos).reshape(bsz, L, N_ATTN_GROUPS, ATTN_HPG, HEAD_DIM)
    v = v.reshape(bsz, L, N_ATTN_GROUPS, ATTN_HPG, HEAD_DIM)
    outs, lses, new_kv = [], [], []
    for g in range(N_ATTN_GROUPS):
        W, dil = WINDOWS[g], DILATIONS[g]
        qg, kg, vg = q[:, :, g], k[:, :, g], v[:, :, g]
        if is_sample:
            buf = kv_bufs[g].astype(kg.dtype)
            lwin = buf.shape[1]
            k_all = jnp.concatenate([buf[:, :, 0], kg], axis=1)
            v_all = jnp.concatenate([buf[:, :, 1], vg], axis=1)
            o, l = dilated_attn_sample(qg, k_all, v_all, lwin, W, dil)
            new_kv.append(jnp.stack([k_all, v_all], axis=2)[:, -lwin:])
        else:
            o, l = dilated_attn_prompt(qg, kg, vg, W, dil)
            new_kv.append(jnp.stack([kg, vg], axis=2)[:, -min(W, L):])
        outs.append(o)
        lses.append(l)
    alpha = jax.nn.softmax(jnp.stack(lses), axis=0)
    o_attn = jnp.sum(alpha[..., None] * jnp.stack(outs), axis=0).astype(x.dtype).reshape(bsz, L, ATTN_OUT_DIM)

    xpad = jnp.concatenate([conv_buf.astype(xbc.dtype), xbc], axis=1)
    conv = cbias
    for j in range(CONV_WIDTH):
        conv = conv + xpad[:, j:j + L] * cw[j]
    new_conv = xpad[:, -(CONV_WIDTH - 1):]
    xs, bm, cm = split_cols(jax.nn.silu(conv), (SSM_INNER, SSM_GROUPS * SSM_STATE, SSM_GROUPS * SSM_STATE))
    dt = jax.nn.softplus(dt_raw.astype(jnp.float32) + dtb.astype(jnp.float32))
    a = -jnp.exp(alog.astype(jnp.float32))
    xh = xs.reshape(bsz, L, SSM_HEADS, SSM_HEAD_DIM)
    y, h_new = ssd_scan(xh, dt, a, bm.reshape(bsz, L, SSM_GROUPS, SSM_STATE),
                        cm.reshape(bsz, L, SSM_GROUPS, SSM_STATE), ssm_h)
    y = y + dsk.astype(jnp.float32)[:, None] * xh.astype(jnp.float32)
    yg = (y.reshape(bsz, L, SSM_INNER) * jax.nn.silu(z.astype(jnp.float32))).reshape(bsz, L, SSM_GROUPS, SSM_INNER // SSM_GROUPS)
    yg = yg * lax.rsqrt(jnp.mean(yg * yg, axis=-1, keepdims=True) + NORM_EPS)
    y_ssm = (yg.reshape(bsz, L, SSM_INNER) * snw.astype(jnp.float32)).astype(x.dtype)

    merged = jax.nn.sigmoid(g_a) * (o_attn @ wao) + jax.nn.sigmoid(g_b) * (y_ssm @ wso)
    x = x + merged @ wo

    hdn = jnp.square(jax.nn.relu(rms_norm(x, n2) @ wup))
    x = x + hdn @ wdn

    x = x + jax.nn.sigmoid(x @ wpg) * (ple.astype(x.dtype) @ wpp)
    return x, new_kv, new_conv, h_new.astype(x.dtype)


def setup_inputs(seed: int = 0) -> dict:
    key = jax.random.key(seed)
    ks = list(jax.random.split(key, 40))

    def nrm(i, shape, scale):
        return jax.random.normal(ks[i], shape, jnp.float32) * scale

    def kv_shape(W):
        return (DEPTH, DEC_BATCH, min(W, PAST_LEN), 2, ATTN_HPG, HEAD_DIM)

    dt0 = jnp.exp(jax.random.uniform(ks[30], (DEPTH, SSM_HEADS), jnp.float32,
                                     minval=math.log(DT_MIN), maxval=math.log(DT_MAX)))
    dt_bias = dt0 + jnp.log(-jnp.expm1(-dt0))
    a_log = jnp.log(jax.random.uniform(ks[31], (DEPTH, SSM_HEADS), jnp.float32, minval=1.0, maxval=16.0))
    return {
        'x_prompt': nrm(0, (BATCH, SEQ, D_MODEL), 1.0),
        'x_sample': nrm(1, (DEC_BATCH, DEC_SEQ, D_MODEL), 1.0),
        'cache_kv_w128': nrm(2, kv_shape(WINDOWS[0]), 1.0),
        'cache_kv_w512': nrm(3, kv_shape(WINDOWS[1]), 1.0),
        'cache_kv_w2048': nrm(4, kv_shape(WINDOWS[2]), 1.0),
        'state_ssm': nrm(5, (DEPTH, DEC_BATCH, SSM_HEADS, SSM_HEAD_DIM, SSM_STATE), 0.1),
        'state_conv': nrm(6, (DEPTH, DEC_BATCH, CONV_WIDTH - 1, XBC_DIM), 1.0),
        'p_prompt': nrm(7, (DEPTH, BATCH, SEQ, PLE_DIM), 1.0),
        'p_sample': nrm(8, (DEPTH, DEC_BATCH, DEC_SEQ, PLE_DIM), 1.0),
        'norm1_w': 1.0 + nrm(9, (DEPTH, D_MODEL), 0.01),
        'w_in': nrm(10, (DEPTH, D_MODEL, IN_DIM), D_MODEL ** -0.5),
        'conv_w': nrm(11, (DEPTH, CONV_WIDTH, XBC_DIM), CONV_WIDTH ** -0.5),
        'conv_b': nrm(12, (DEPTH, XBC_DIM), 0.01),
        'dt_bias': dt_bias,
        'a_log': a_log,
        'd_skip': 1.0 + nrm(13, (DEPTH, SSM_HEADS), 0.01),
        'ssm_norm_w': 1.0 + nrm(14, (DEPTH, SSM_INNER), 0.01),
        'w_attn_out': nrm(15, (DEPTH, ATTN_OUT_DIM, D_MODEL), ATTN_OUT_DIM ** -0.5),
        'w_ssm_out': nrm(16, (DEPTH, SSM_INNER, D_MODEL), SSM_INNER ** -0.5),
        'w_o': nrm(17, (DEPTH, D_MODEL, D_MODEL), 0.5 * D_MODEL ** -0.5),
        'norm2_w': 1.0 + nrm(18, (DEPTH, D_MODEL), 0.01),
        'w_up': nrm(19, (DEPTH, D_MODEL, D_FF), D_MODEL ** -0.5),
        'w_down': nrm(20, (DEPTH, D_FF, D_MODEL), 0.5 * D_FF ** -0.5),
        'w_ple_proj': nrm(21, (DEPTH, PLE_DIM, D_MODEL), 0.5 * PLE_DIM ** -0.5),
        'w_ple_gate': nrm(22, (DEPTH, D_MODEL, D_MODEL), D_MODEL ** -0.5),
        'final_norm_w': 1.0 + nrm(23, (D_MODEL,), 0.01),
    }


def reference(x_prompt, x_sample, cache_kv_w128, cache_kv_w512, cache_kv_w2048, state_ssm, state_conv,
              p_prompt, p_sample, norm1_w, w_in, conv_w, conv_b, dt_bias, a_log, d_skip, ssm_norm_w,
              w_attn_out, w_ssm_out, w_o, norm2_w, w_up, w_down, w_ple_proj, w_ple_gate, final_norm_w):
    bp, lp = x_prompt.shape[0], x_prompt.shape[1]
    pos_p = jnp.arange(lp)
    pos_s = PAST_LEN + jnp.arange(x_sample.shape[1])
    kv_caches = (cache_kv_w128, cache_kv_w512, cache_kv_w2048)
    hp, hs = x_prompt, x_sample
    kv_p = ([], [], [])
    kv_s = ([], [], [])
    ssm_p, ssm_s, conv_p, conv_s = [], [], [], []
    for i in range(DEPTH):
        lw = (norm1_w[i], w_in[i], conv_w[i], conv_b[i], dt_bias[i], a_log[i], d_skip[i], ssm_norm_w[i],
              w_attn_out[i], w_ssm_out[i], w_o[i], norm2_w[i], w_up[i], w_down[i], w_ple_proj[i], w_ple_gate[i])
        zero_conv = jnp.zeros((bp, CONV_WIDTH - 1, XBC_DIM), x_prompt.dtype)
        zero_ssm = jnp.zeros((bp, SSM_HEADS, SSM_HEAD_DIM, SSM_STATE), jnp.float32)
        hp, nkv, ncv, nss = block(hp, p_prompt[i], pos_p, None, zero_conv, zero_ssm, lw, False)
        for g in range(N_ATTN_GROUPS):
            kv_p[g].append(nkv[g])
        conv_p.append(ncv)
        ssm_p.append(nss)
        hs, nkv, ncv, nss = block(hs, p_sample[i], pos_s, tuple(c[i] for c in kv_caches),
                                  state_conv[i], state_ssm[i], lw, True)
        for g in range(N_ATTN_GROUPS):
            kv_s[g].append(nkv[g])
        conv_s.append(ncv)
        ssm_s.append(nss)
    y_prompt = rms_norm(hp, final_norm_w)
    y_sample = rms_norm(hs, final_norm_w)
    return (y_prompt, y_sample,
            jnp.stack(kv_p[0]), jnp.stack(kv_p[1]), jnp.stack(kv_p[2]), jnp.stack(ssm_p), jnp.stack(conv_p),
            jnp.stack(kv_s[0]), jnp.stack(kv_s[1]), jnp.stack(kv_s[2]), jnp.stack(ssm_s), jnp.stack(conv_s))
```

```python
import functools
import math

import jax
import jax.numpy as jnp
from jax import lax
from jax.experimental import pallas as pl
from jax.experimental.pallas import tpu as pltpu

F32 = jnp.float32
BF16 = jnp.bfloat16

D_MODEL = 1024
HEAD_DIM = 64
ATTN_HPG = 4
GROUP_W = ATTN_HPG * HEAD_DIM
WINDOWS = (128, 512, 2048)
DILATIONS = (1, 4, 16)
N_GROUPS = 3
KEYS_PER_QUERY_BLOCK = 128
ROPE_THETA = 10000.0
PAST_LEN = 2048
DEC_SEQ = 4

SSM_INNER = 2048
SSM_HEADS = 32
SSM_HEAD_DIM = 64
SSM_GROUPS = 4
SSM_HPG = SSM_HEADS // SSM_GROUPS
SSM_STATE = 128
SSM_GROUP_W = SSM_INNER // SSM_GROUPS
CONV_WIDTH = 4
XBC_DIM = SSM_INNER + 2 * SSM_GROUPS * SSM_STATE
SSD_CHUNK = 128
D_FF = 4 * D_MODEL
PLE_DIM = 256
NORM_EPS = 1e-6
NEG_BIG = -1e30

QK_W = 2 * N_GROUPS * GROUP_W
REST_XBC = 0
REST_V = XBC_DIM
REST_DT = REST_V + N_GROUPS * GROUP_W
REST_Z = 4096
REST_G = REST_Z + SSM_INNER
REST_W = REST_G + 2 * D_MODEL
DT_PAD = 256
LANES = 128
DECAY_ROWS = 16

VMEM_LIMIT = 56 * 1024 * 1024


def _cparams(*sem):
    return pltpu.CompilerParams(dimension_semantics=sem, vmem_limit_bytes=VMEM_LIMIT)


def _sigmoid(x):
    return 1.0 / (1.0 + jnp.exp(-x))


def _softplus(x):
    return jnp.maximum(x, 0.0) + jnp.log(1.0 + jnp.exp(-jnp.abs(x)))


def _split3(x):
    hi = x.astype(BF16)
    r1 = x - hi.astype(F32)
    mid = r1.astype(BF16)
    lo = (r1 - mid.astype(F32)).astype(BF16)
    return hi, mid, lo


def _dot_sel(sel_bf16, x):
    hi, mid, lo = _split3(x)
    out = jnp.dot(sel_bf16, hi, preferred_element_type=F32)
    out = out + jnp.dot(sel_bf16, mid, preferred_element_type=F32)
    return out + jnp.dot(sel_bf16, lo, preferred_element_type=F32)


def _expand_heads(v, e_bf16):
    hi, mid, lo = _split3(v)
    out = jnp.dot(hi, e_bf16, preferred_element_type=F32)
    out = out + jnp.dot(mid, e_bf16, preferred_element_type=F32)
    return out + jnp.dot(lo, e_bf16, preferred_element_type=F32)


def _inproj_body(x_ref, nw_ref, w_ref, *rest, rope, tn):
    if rope:
        cos_ref, sin_ref, o_ref, u_ref = rest
    else:
        o_ref, u_ref = rest

    @pl.when(pl.program_id(1) == 0)
    def _():
        x = x_ref[...]
        ms = jnp.mean(x * x, axis=-1, keepdims=True)
        u_ref[...] = (x * lax.rsqrt(ms + NORM_EPS) * nw_ref[...]).astype(BF16)

    acc = jnp.dot(u_ref[...], w_ref[...], preferred_element_type=F32)
    if rope:
        cos = cos_ref[...]
        sin = sin_ref[...]
        lane = lax.broadcasted_iota(jnp.int32, cos.shape, 1)
        first_half = (lane % HEAD_DIM) < (HEAD_DIM // 2)
        for c in range(tn // LANES):
            t = acc[:, c * LANES:(c + 1) * LANES]
            rot = jnp.where(first_half, pltpu.roll(t, LANES - HEAD_DIM // 2, 1), pltpu.roll(t, HEAD_DIM // 2, 1))
            o_ref[:, c * LANES:(c + 1) * LANES] = t * cos + rot * sin
    else:
        o_ref[...] = acc


def _inproj(x, nw, w, cos=None, sin=None, *, tm, tn):
    n, d = x.shape
    nout = w.shape[1]
    rope = cos is not None
    in_specs = [
        pl.BlockSpec((tm, d), lambda i, j: (i, 0)),
        pl.BlockSpec((1, d), lambda i, j: (0, 0)),
        pl.BlockSpec((d, tn), lambda i, j: (0, j)),
    ]
    args = [x, nw, w]
    if rope:
        in_specs += [pl.BlockSpec((tm, LANES), lambda i, j: (i, 0)), pl.BlockSpec((tm, LANES), lambda i, j: (i, 0))]
        args += [cos, sin]
    return pl.pallas_call(
        functools.partial(_inproj_body, rope=rope, tn=tn),
        grid=(n // tm, nout // tn),
        in_specs=in_specs,
        out_specs=pl.BlockSpec((tm, tn), lambda i, j: (i, j)),
        out_shape=jax.ShapeDtypeStruct((n, nout), F32),
        scratch_shapes=[pltpu.VMEM((tm, d), BF16)],
        compiler_params=_cparams("parallel", "arbitrary"),
    )(*args)


def _attn_prompt_body(q_ref, kp_ref, kc_ref, vp_ref, vc_ref, o_ref, l_ref):
    blk = KEYS_PER_QUERY_BLOCK
    n = pl.program_id(2)
    q = q_ref[...] * (HEAD_DIM ** -0.5)
    kk = jnp.concatenate([kp_ref[...], kc_ref[...]], axis=0).astype(BF16)
    vv = jnp.concatenate([vp_ref[...], vc_ref[...]], axis=0).astype(BF16)
    qi = lax.broadcasted_iota(jnp.int32, (blk, 2 * blk), 0)
    ki = lax.broadcasted_iota(jnp.int32, (blk, 2 * blk), 1)
    dist = blk + qi - ki
    mask = (dist >= 0) & (dist <= blk) & ((ki >= blk) | (n > 0))
    lane_head = lax.broadcasted_iota(jnp.int32, (blk, GROUP_W), 1) // HEAD_DIM
    o = jnp.zeros((blk, GROUP_W), F32)
    lse = jnp.zeros((blk, GROUP_W), F32)
    for h in range(ATTN_HPG):
        hm = lane_head == h
        qh = jnp.where(hm, q, 0.0).astype(BF16)
        s = lax.dot_general(qh, kk, (((1,), (1,)), ((), ())), preferred_element_type=F32)
        s = jnp.where(mask, s, NEG_BIG)
        m = jnp.max(s, axis=-1, keepdims=True)
        p = jnp.exp(s - m)
        l = jnp.sum(p, axis=-1, keepdims=True)
        oh = jnp.dot(p.astype(BF16), vv, preferred_element_type=F32)
        o = jnp.where(hm, oh / l, o)
        lse = jnp.where(hm, m + jnp.log(l), lse)
    o_ref[...] = o
    l_ref[...] = lse


def _attn_prompt(qk, rest, g, *, batch, seq):
    n = qk.shape[0]
    d = DILATIONS[g]
    blk = KEYS_PER_QUERY_BLOCK
    nbt = seq // (d * blk)
    qk_v = qk.reshape(n // d, d * QK_W)
    rest_v = rest.reshape(n // d, d * REST_W)
    qpb = QK_W // GROUP_W
    rpb = REST_W // GROUP_W
    q_col, k_col, v_col = g, N_GROUPS + g, REST_V // GROUP_W + g

    def cur(col, per):
        return lambda b, r, i: (b * nbt + i, r * per + col)

    def prev(col, per):
        return lambda b, r, i: (b * nbt + jnp.maximum(i - 1, 0), r * per + col)

    bs = (blk, GROUP_W)
    out_sds = jax.ShapeDtypeStruct((batch * seq // d, d * GROUP_W), F32)
    o, lse = pl.pallas_call(
        _attn_prompt_body,
        grid=(batch, d, nbt),
        in_specs=[
            pl.BlockSpec(bs, cur(q_col, qpb)),
            pl.BlockSpec(bs, prev(k_col, qpb)),
            pl.BlockSpec(bs, cur(k_col, qpb)),
            pl.BlockSpec(bs, prev(v_col, rpb)),
            pl.BlockSpec(bs, cur(v_col, rpb)),
        ],
        out_specs=[pl.BlockSpec(bs, lambda b, r, i: (b * nbt + i, r))] * 2,
        out_shape=[out_sds, out_sds],
        compiler_params=_cparams("parallel", "parallel", "arbitrary"),
    )(qk_v, qk_v, qk_v, rest_v, rest_v)
    return o.reshape(batch * seq, GROUP_W), lse.reshape(batch * seq, GROUP_W)


def _attn_sample_body(q_ref, kn_ref, vn_ref, *rest, d, bt):
    n_res = 1 if d == 1 else DEC_SEQ
    cache_refs = rest[:n_res]
    o_ref, l_ref = rest[n_res:]
    nk = KEYS_PER_QUERY_BLOCK
    rows = DEC_SEQ * ATTN_HPG
    row_tok = lax.broadcasted_iota(jnp.int32, (rows, 1), 0) // ATTN_HPG
    row_head = lax.broadcasted_iota(jnp.int32, (rows, GROUP_W), 0) % ATTN_HPG
    lane_head = lax.broadcasted_iota(jnp.int32, (rows, GROUP_W), 1) // HEAD_DIM
    own = lane_head == row_head
    key_row = lax.broadcasted_iota(jnp.int32, (rows, nk), 1)
    cache_valid = key_row >= (row_tok // d)
    new_valid = [(row_tok >= jj) & (((row_tok - jj) % d) == 0) for jj in range(DEC_SEQ)]
    o_out, l_out = [], []
    for bi in range(bt):
        sl = slice(bi * DEC_SEQ, (bi + 1) * DEC_SEQ)
        qb = q_ref[sl, :] * (HEAD_DIM ** -0.5)
        knew = kn_ref[sl, :]
        vnew = vn_ref[sl, :]
        qrep = jnp.concatenate([jnp.broadcast_to(qb[j:j + 1, :], (ATTN_HPG, GROUP_W)) for j in range(DEC_SEQ)], axis=0)
        qrows = jnp.where(own, qrep, 0.0)
        qrows_b = qrows.astype(BF16)
        if d == 1:
            kc = cache_refs[0][bi, :, 0:GROUP_W].astype(BF16)
            s_c = lax.dot_general(qrows_b, kc, (((1,), (1,)), ((), ())), preferred_element_type=F32)
        else:
            s_c = jnp.zeros((rows, nk), F32)
            for j in range(DEC_SEQ):
                kc = cache_refs[j][bi, :, 0:GROUP_W].astype(BF16)
                sj = lax.dot_general(qrows_b, kc, (((1,), (1,)), ((), ())), preferred_element_type=F32)
                s_c = jnp.where(row_tok == j, sj, s_c)
        s_c = jnp.where(cache_valid, s_c, NEG_BIG)
        s_n = [jnp.where(new_valid[jj], jnp.sum(qrows * knew[jj:jj + 1, :], axis=-1, keepdims=True), NEG_BIG)
               for jj in range(DEC_SEQ)]
        m = jnp.max(s_c, axis=-1, keepdims=True)
        for jj in range(DEC_SEQ):
            m = jnp.maximum(m, s_n[jj])
        p_c = jnp.exp(s_c - m)
        l = jnp.sum(p_c, axis=-1, keepdims=True)
        p_cb = p_c.astype(BF16)
        if d == 1:
            vc = cache_refs[0][bi, :, GROUP_W:2 * GROUP_W].astype(BF16)
            acc = jnp.dot(p_cb, vc, preferred_element_type=F32)
        else:
            acc = jnp.zeros((rows, GROUP_W), F32)
            for j in range(DEC_SEQ):
                vc = cache_refs[j][bi, :, GROUP_W:2 * GROUP_W].astype(BF16)
                aj = jnp.dot(p_cb, vc, preferred_element_type=F32)
                acc = jnp.where(row_tok == j, aj, acc)
        for jj in range(DEC_SEQ):
            p_n = jnp.exp(s_n[jj] - m)
            l = l + p_n
            acc = acc + p_n * vnew[jj:jj + 1, :]
        o_rows = jnp.where(own, acc / l, 0.0)
        l_rows = jnp.where(own, m + jnp.log(l), 0.0)
        for j in range(DEC_SEQ):
            rs = slice(j * ATTN_HPG, (j + 1) * ATTN_HPG)
            o_out.append(jnp.sum(o_rows[rs, :], axis=0, keepdims=True))
            l_out.append(jnp.sum(l_rows[rs, :], axis=0, keepdims=True))
    o_ref[...] = jnp.concatenate(o_out, axis=0)
    l_ref[...] = jnp.concatenate(l_out, axis=0)


def _attn_sample(qk, rest, cache, g, *, n_prompt, dec_batch, bt=8):
    d = DILATIONS[g]
    w = cache.shape[1]
    assert w == WINDOWS[g] and (d == 1 or d >= DEC_SEQ)
    nk = w // d
    cache_v = cache.reshape(dec_batch, nk, d * 2 * GROUP_W)
    rt = bt * DEC_SEQ
    row0 = n_prompt // rt
    n_res = 1 if d == 1 else DEC_SEQ
    rb = (rt, GROUP_W)
    out_sds = jax.ShapeDtypeStruct((dec_batch * DEC_SEQ, GROUP_W), F32)
    return pl.pallas_call(
        functools.partial(_attn_sample_body, d=d, bt=bt),
        grid=(dec_batch // bt,),
        in_specs=[
            pl.BlockSpec(rb, lambda i: (row0 + i, g)),
            pl.BlockSpec(rb, lambda i: (row0 + i, N_GROUPS + g)),
            pl.BlockSpec(rb, lambda i: (row0 + i, REST_V // GROUP_W + g)),
        ] + [pl.BlockSpec((bt, nk, 2 * GROUP_W), functools.partial(lambda i, j: (i, 0, j), j=j)) for j in range(n_res)],
        out_specs=[pl.BlockSpec(rb, lambda i: (i, 0))] * 2,
        out_shape=[out_sds, out_sds],
        compiler_params=_cparams("parallel"),
    )(qk, qk, rest, *([cache_v] * n_res))


def _gated_group_norm(y, z, snw):
    yg = y * (z * _sigmoid(z))
    outs = []
    for g in range(SSM_GROUPS):
        seg = yg[:, g * SSM_GROUP_W:(g + 1) * SSM_GROUP_W]
        ms = jnp.mean(seg * seg, axis=-1, keepdims=True)
        outs.append(seg * lax.rsqrt(ms + NORM_EPS))
    return (jnp.concatenate(outs, axis=-1) * snw).astype(BF16)


def _decay_rows(cd_row):
    hi = cd_row.astype(BF16).astype(F32)
    r1 = cd_row - hi
    mid = r1.astype(BF16).astype(F32)
    lo = r1 - mid
    zeros = jnp.zeros((DECAY_ROWS - 3, cd_row.shape[1]), F32)
    return jnp.concatenate([hi, mid, lo, zeros], axis=0).astype(BF16)


def _ssd_prompt_body(xbc_ref, z_ref, dt_ref, cw_ref, cb_ref, dtb_ref, alog_ref, dsk_ref, snw_ref, e_ref, tri_ref,
                     y_ref, h_ref, tail_ref, xpad_ref):
    q = SSD_CHUNK
    c = pl.program_id(1)

    @pl.when(c == 0)
    def _():
        h_ref[...] = jnp.zeros_like(h_ref)
        tail_ref[...] = jnp.zeros_like(tail_ref)

    x = xbc_ref[...]
    xpad_ref[0:8, :] = tail_ref[...]
    xpad_ref[8:8 + q, :] = x
    tail_ref[...] = x[q - 8:q, :]
    conv = cb_ref[...]
    for j in range(CONV_WIDTH):
        off = 8 - (CONV_WIDTH - 1) + j
        conv = conv + xpad_ref[off:off + q, :] * cw_ref[j:j + 1, :]
    act = conv * _sigmoid(conv)
    xs = act[:, :SSM_INNER]
    bm = act[:, SSM_INNER:SSM_INNER + SSM_GROUPS * SSM_STATE]
    cm = act[:, SSM_INNER + SSM_GROUPS * SSM_STATE:]

    e_mat = e_ref[...]
    dt = _softplus(dt_ref[:, 0:LANES] + dtb_ref[...])
    a = -jnp.exp(alog_ref[...])
    da = dt * a
    acum = _dot_sel(tri_ref[...], da)
    acum_t = acum.T
    alast = acum[q - 1:q, :]
    dt_x = _expand_heads(dt, e_mat)
    de_x = _expand_heads(jnp.exp(alast - acum), e_mat)
    ea_x = _expand_heads(jnp.exp(acum), e_mat)
    cd_x = _expand_heads(jnp.broadcast_to(jnp.exp(alast), (8, LANES)), e_mat)[0:1, :]

    xdt = xs * dt_x
    xdt_b = xdt.astype(BF16)
    xsd_b = (xdt * de_x).astype(BF16)
    li = lax.broadcasted_iota(jnp.int32, (q, q), 0)
    si = lax.broadcasted_iota(jnp.int32, (q, q), 1)
    causal = li >= si
    y_parts, st_parts = [], []
    for g in range(SSM_GROUPS):
        bg = bm[:, g * SSM_STATE:(g + 1) * SSM_STATE].astype(BF16)
        cg = cm[:, g * SSM_STATE:(g + 1) * SSM_STATE].astype(BF16)
        cb = lax.dot_general(cg, bg, (((1,), (1,)), ((), ())), preferred_element_type=F32)
        hg = h_ref[0, g * SSM_GROUP_W:(g + 1) * SSM_GROUP_W, :].astype(BF16)
        y_off = lax.dot_general(cg, hg, (((1,), (1,)), ((), ())), preferred_element_type=F32)
        yd = []
        for e in range(SSM_HPG):
            he = g * SSM_HPG + e
            seg = acum[:, he:he + 1] - acum_t[he:he + 1, :]
            lm = jnp.exp(jnp.where(causal, seg, NEG_BIG))
            mm = (cb * lm).astype(BF16)
            yd.append(jnp.dot(mm, xdt_b[:, he * SSM_HEAD_DIM:(he + 1) * SSM_HEAD_DIM], preferred_element_type=F32))
        y_parts.append(jnp.concatenate(yd, axis=-1) + y_off * ea_x[:, g * SSM_GROUP_W:(g + 1) * SSM_GROUP_W])
        st_parts.append(lax.dot_general(xsd_b[:, g * SSM_GROUP_W:(g + 1) * SSM_GROUP_W], bg,
                                        (((0,), (0,)), ((), ())), preferred_element_type=F32))
    y = jnp.concatenate(y_parts, axis=-1) + dsk_ref[...] * xs
    dmat = lax.dot_general(_decay_rows(cd_x), jnp.ones((DECAY_ROWS, SSM_STATE), BF16), (((0,), (0,)), ((), ())),
                           preferred_element_type=F32)
    h_ref[0] = h_ref[0] * dmat + jnp.concatenate(st_parts, axis=0)
    y_ref[...] = _gated_group_norm(y, z_ref[...], snw_ref[...])


def _ssd_consts():
    e = (jnp.arange(LANES)[:, None] == (jnp.arange(SSM_INNER)[None, :] // SSM_HEAD_DIM)).astype(BF16)
    tri = (jnp.arange(SSD_CHUNK)[:, None] >= jnp.arange(SSD_CHUNK)[None, :]).astype(BF16)
    return e, tri


def _ssd_param_specs(nd):
    zero = (lambda *_: (0, 0))
    del nd
    return [
        pl.BlockSpec((CONV_WIDTH, XBC_DIM), zero),
        pl.BlockSpec((1, XBC_DIM), zero),
        pl.BlockSpec((1, LANES), zero),
        pl.BlockSpec((1, LANES), zero),
        pl.BlockSpec((1, SSM_INNER), zero),
        pl.BlockSpec((1, SSM_INNER), zero),
        pl.BlockSpec((LANES, SSM_INNER), zero),
    ]


def _ssd_prompt(rest, params, *, batch, seq):
    q = SSD_CHUNK
    nc = seq // q
    e_mat, tri = _ssd_consts()
    row = lambda b, c: b * nc + c
    return pl.pallas_call(
        _ssd_prompt_body,
        grid=(batch, nc),
        in_specs=[
            pl.BlockSpec((q, XBC_DIM), lambda b, c: (row(b, c), REST_XBC // XBC_DIM)),
            pl.BlockSpec((q, SSM_INNER), lambda b, c: (row(b, c), REST_Z // SSM_INNER)),
            pl.BlockSpec((q, DT_PAD), lambda b, c: (row(b, c), REST_DT // DT_PAD)),
        ] + _ssd_param_specs(2) + [pl.BlockSpec((q, q), lambda b, c: (0, 0))],
        out_specs=[
            pl.BlockSpec((q, SSM_INNER), lambda b, c: (row(b, c), 0)),
            pl.BlockSpec((1, SSM_INNER, SSM_STATE), lambda b, c: (b, 0, 0)),
        ],
        out_shape=[
            jax.ShapeDtypeStruct((batch * seq, SSM_INNER), BF16),
            jax.ShapeDtypeStruct((batch, SSM_INNER, SSM_STATE), F32),
        ],
        scratch_shapes=[pltpu.VMEM((8, XBC_DIM), F32), pltpu.VMEM((8 + q, XBC_DIM), F32)],
        compiler_params=_cparams("parallel", "arbitrary"),
    )(rest, rest, rest, *params, e_mat, tri)


def _ssd_sample_body(xbc_ref, z_ref, dt_ref, cs_ref, h0_ref, cw_ref, cb_ref, dtb_ref, alog_ref, dsk_ref, snw_ref,
                     e_ref, y_ref, h_ref, *, bt):
    r = bt * DEC_SEQ
    lrow = lax.broadcasted_iota(jnp.int32, (r, 1), 0) % DEC_SEQ
    rowb = lax.broadcasted_iota(jnp.int32, (r, 1), 0) // DEC_SEQ

    def shift(v, k):
        return v if k == 0 else pltpu.roll(v, k, 0)

    x = xbc_ref[...]
    cs = cs_ref[...]
    conv = cb_ref[...]
    for j in range(CONV_WIDTH):
        k = CONV_WIDTH - 1 - j
        if k == 0:
            term = x
        else:
            carried = cs if k == CONV_WIDTH - 1 else pltpu.roll(cs, r - (CONV_WIDTH - 1 - k), 0)
            term = jnp.where(lrow >= k, shift(x, k), carried)
        conv = conv + term * cw_ref[j:j + 1, :]
    act = conv * _sigmoid(conv)
    xs = act[:, :SSM_INNER]
    bm = act[:, SSM_INNER:SSM_INNER + SSM_GROUPS * SSM_STATE]
    cm = act[:, SSM_INNER + SSM_GROUPS * SSM_STATE:]

    e_mat = e_ref[...]
    dt = _softplus(dt_ref[:, 0:LANES] + dtb_ref[...])
    a = -jnp.exp(alog_ref[...])
    da = dt * a
    acum = da
    for k in range(1, DEC_SEQ):
        acum = acum + jnp.where(lrow >= k, shift(da, k), 0.0)
    alast = acum
    for k in range(1, DEC_SEQ):
        alast = jnp.where(lrow == DEC_SEQ - 1 - k, pltpu.roll(acum, r - k, 0), alast)

    lane_grp = lax.broadcasted_iota(jnp.int32, (r, LANES), 1) // SSM_HPG
    g_rows = []
    for k in range(DEC_SEQ):
        bk = shift(bm, k)
        cbk = jnp.zeros((r, LANES), F32)
        for g in range(SSM_GROUPS):
            sl = slice(g * SSM_STATE, (g + 1) * SSM_STATE)
            cbk = jnp.where(lane_grp == g, jnp.sum(cm[:, sl] * bk[:, sl], axis=-1, keepdims=True), cbk)
        g_rows.append(jnp.where(lrow >= k, cbk * jnp.exp(acum - shift(acum, k)), 0.0))
    stacked = jnp.concatenate([dt, jnp.exp(alast - acum), jnp.exp(acum), jnp.exp(alast)] + g_rows, axis=0)
    ex = _expand_heads(stacked, e_mat)
    dt_x, de_x, ea_x, cd_x = ex[0:r], ex[r:2 * r], ex[2 * r:3 * r], ex[3 * r:4 * r]

    xdt = xs * dt_x
    y = dsk_ref[...] * xs
    for k in range(DEC_SEQ):
        y = y + ex[(4 + k) * r:(5 + k) * r] * shift(xdt, k)

    xsd = xdt * de_x
    ones = jnp.ones((DECAY_ROWS, SSM_STATE), BF16)
    y_off = jnp.zeros((r, SSM_INNER), F32)
    for bi in range(bt):
        mine = rowb == bi
        xsd_b = jnp.where(mine, xsd, 0.0).astype(BF16)
        h0 = h0_ref[bi]
        yo, st = [], []
        for g in range(SSM_GROUPS):
            gs = slice(g * SSM_GROUP_W, (g + 1) * SSM_GROUP_W)
            bg = bm[:, g * SSM_STATE:(g + 1) * SSM_STATE].astype(BF16)
            cg = cm[:, g * SSM_STATE:(g + 1) * SSM_STATE].astype(BF16)
            yo.append(lax.dot_general(cg, h0[gs, :].astype(BF16), (((1,), (1,)), ((), ())), preferred_element_type=F32))
            st.append(lax.dot_general(xsd_b[:, gs], bg, (((0,), (0,)), ((), ())), preferred_element_type=F32))
        y_off = jnp.where(mine, jnp.concatenate(yo, axis=-1), y_off)
        dmat = lax.dot_general(_decay_rows(cd_x[bi * DEC_SEQ:bi * DEC_SEQ + 1, :]), ones, (((0,), (0,)), ((), ())),
                               preferred_element_type=F32)
        h_ref[bi] = h0 * dmat + jnp.concatenate(st, axis=0)
    y = y + y_off * ea_x
    y_ref[...] = _gated_group_norm(y, z_ref[...], snw_ref[...])


def _ssd_sample(rest, conv_state, h0, params, *, n_prompt, dec_batch, bt=4):
    r = bt * DEC_SEQ
    row0 = n_prompt // r
    e_mat, _ = _ssd_consts()
    return pl.pallas_call(
        functools.partial(_ssd_sample_body, bt=bt),
        grid=(dec_batch // bt,),
        in_specs=[
            pl.BlockSpec((r, XBC_DIM), lambda i: (row0 + i, REST_XBC // XBC_DIM)),
            pl.BlockSpec((r, SSM_INNER), lambda i: (row0 + i, REST_Z // SSM_INNER)),
            pl.BlockSpec((r, DT_PAD), lambda i: (row0 + i, REST_DT // DT_PAD)),
            pl.BlockSpec((r, XBC_DIM), lambda i: (i, 0)),
            pl.BlockSpec((bt, SSM_INNER, SSM_STATE), lambda i: (i, 0, 0)),
        ] + _ssd_param_specs(1),
        out_specs=[
            pl.BlockSpec((r, SSM_INNER), lambda i: (i, 0)),
            pl.BlockSpec((bt, SSM_INNER, SSM_STATE), lambda i: (i, 0, 0)),
        ],
        out_shape=[
            jax.ShapeDtypeStruct((dec_batch * DEC_SEQ, SSM_INNER), BF16),
            jax.ShapeDtypeStruct((dec_batch, SSM_INNER, SSM_STATE), F32),
        ],
        compiler_params=_cparams("parallel"),
    )(rest, rest, rest, conv_state, h0, *params, e_mat)


def _merge_body(x_ref, o0_ref, o1_ref, o2_ref, l0_ref, l1_ref, l2_ref, y_ref, g_ref, wao_ref, wso_ref, wo_ref, out_ref):
    l0, l1, l2 = l0_ref[...], l1_ref[...], l2_ref[...]
    m = jnp.maximum(jnp.maximum(l0, l1), l2)
    e0, e1, e2 = jnp.exp(l0 - m), jnp.exp(l1 - m), jnp.exp(l2 - m)
    o_attn = (e0 * o0_ref[...] + e1 * o1_ref[...] + e2 * o2_ref[...]) / (e0 + e1 + e2)
    att = jnp.dot(o_attn.astype(BF16), wao_ref[...], preferred_element_type=F32)
    ssm = jnp.dot(y_ref[...], wso_ref[...], preferred_element_type=F32)
    merged = _sigmoid(g_ref[:, :D_MODEL]) * att + _sigmoid(g_ref[:, D_MODEL:]) * ssm
    out_ref[...] = x_ref[...] + jnp.dot(merged.astype(BF16), wo_ref[...], preferred_element_type=F32)


def _merge(x, os, ls, y_ssm, rest, wao, wso, wo, *, tm):
    n = x.shape[0]
    row = lambda i: (i, 0)
    fixed = lambda i: (0, 0)
    gspec = pl.BlockSpec((tm, GROUP_W), row)
    return pl.pallas_call(
        _merge_body,
        grid=(n // tm,),
        in_specs=[pl.BlockSpec((tm, D_MODEL), row)] + [gspec] * 6 + [
            pl.BlockSpec((tm, SSM_INNER), row),
            pl.BlockSpec((tm, 2 * D_MODEL), lambda i: (i, REST_G // (2 * D_MODEL))),
            pl.BlockSpec(wao.shape, fixed),
            pl.BlockSpec(wso.shape, fixed),
            pl.BlockSpec(wo.shape, fixed),
        ],
        out_specs=pl.BlockSpec((tm, D_MODEL), row),
        out_shape=jax.ShapeDtypeStruct((n, D_MODEL), F32),
        compiler_params=_cparams("parallel"),
    )(x, *os, *ls, y_ssm, rest, wao, wso, wo)


def _mlp_body(x_ref, ple_ref, n2_ref, wup_ref, wdn_ref, wpg_ref, wpp_ref, out_ref):
    x = x_ref[...]
    ms = jnp.mean(x * x, axis=-1, keepdims=True)
    u = (x * lax.rsqrt(ms + NORM_EPS) * n2_ref[...]).astype(BF16)
    hdn = jnp.maximum(jnp.dot(u, wup_ref[...], preferred_element_type=F32), 0.0)
    hdn = (hdn * hdn).astype(BF16)
    x = x + jnp.dot(hdn, wdn_ref[...], preferred_element_type=F32)
    gate = _sigmoid(jnp.dot(x.astype(BF16), wpg_ref[...], preferred_element_type=F32))
    out_ref[...] = x + gate * jnp.dot(ple_ref[...].astype(BF16), wpp_ref[...], preferred_element_type=F32)


def _mlp(x, ple, n2, wup, wdn, wpg, wpp, *, tm):
    n = x.shape[0]
    row = lambda i: (i, 0)
    fixed = lambda i: (0, 0)
    single = pl.Buffered(1)
    return pl.pallas_call(
        _mlp_body,
        grid=(n // tm,),
        in_specs=[
            pl.BlockSpec((tm, D_MODEL), row),
            pl.BlockSpec((tm, PLE_DIM), row),
            pl.BlockSpec((1, D_MODEL), fixed),
            pl.BlockSpec(wup.shape, fixed, pipeline_mode=single),
            pl.BlockSpec(wdn.shape, fixed, pipeline_mode=single),
            pl.BlockSpec(wpg.shape, fixed, pipeline_mode=single),
            pl.BlockSpec(wpp.shape, fixed, pipeline_mode=single),
        ],
        out_specs=pl.BlockSpec((tm, D_MODEL), row),
        out_shape=jax.ShapeDtypeStruct((n, D_MODEL), F32),
        compiler_params=_cparams("parallel"),
    )(x, ple, n2, wup, wdn, wpg, wpp)


def _final_norm_body(x_ref, w_ref, o_ref):
    x = x_ref[...]
    ms = jnp.mean(x * x, axis=-1, keepdims=True)
    o_ref[...] = x * lax.rsqrt(ms + NORM_EPS) * w_ref[...]


def _final_norm(x, w, *, tm):
    n = x.shape[0]
    return pl.pallas_call(
        _final_norm_body,
        grid=(n // tm,),
        in_specs=[pl.BlockSpec((tm, D_MODEL), lambda i: (i, 0)), pl.BlockSpec((1, D_MODEL), lambda i: (0, 0))],
        out_specs=pl.BlockSpec((tm, D_MODEL), lambda i: (i, 0)),
        out_shape=jax.ShapeDtypeStruct((n, D_MODEL), F32),
        compiler_params=_cparams("parallel"),
    )(x, w)


def _rope_tables(seq, batch, dec_batch):
    half = HEAD_DIM // 2
    inv_freq = ROPE_THETA ** (-jnp.arange(half, dtype=F32) / half)
    pos = jnp.concatenate([jnp.tile(jnp.arange(seq), batch), jnp.tile(PAST_LEN + jnp.arange(DEC_SEQ), dec_batch)])
    ang = pos.astype(F32)[:, None] * inv_freq[None, :]
    cos, sin = jnp.cos(ang), jnp.sin(ang)
    reps = LANES // HEAD_DIM
    return jnp.tile(jnp.concatenate([cos, cos], axis=1), (1, reps)), jnp.tile(jnp.concatenate([-sin, sin], axis=1), (1, reps))


def _pad_lanes(v, width):
    return jnp.pad(v, ((0, 0), (0, width - v.shape[1])))


def kernel(x_prompt, x_sample, cache_kv_w128, cache_kv_w512, cache_kv_w2048, state_ssm, state_conv, p_prompt, p_sample, norm1_w, w_in, conv_w, conv_b, dt_bias, a_log, d_skip, ssm_norm_w, w_attn_out, w_ssm_out, w_o, norm2_w, w_up, w_down, w_ple_proj, w_ple_gate, final_norm_w):
    batch, seq, _ = x_prompt.shape
    dec_batch, dec_seq, _ = x_sample.shape
    depth = w_in.shape[0]
    assert dec_seq == DEC_SEQ and seq % (DILATIONS[-1] * KEYS_PER_QUERY_BLOCK) == 0
    n_prompt = batch * seq
    n_sample = dec_batch * dec_seq
    caches = (cache_kv_w128, cache_kv_w512, cache_kv_w2048)
    tm = 512

    c0 = 0
    cols = {}
    for name, width in (("q", 768), ("k", 768), ("v", 768), ("z", SSM_INNER), ("xbc", XBC_DIM), ("dt", SSM_HEADS),
                        ("ga", D_MODEL), ("gb", D_MODEL)):
        cols[name] = (c0, c0 + width)
        c0 += width
    seg = lambda name: w_in[:, :, cols[name][0]:cols[name][1]]
    w_qk = w_in[:, :, :QK_W].astype(BF16)
    w_rest = jnp.concatenate(
        [seg("xbc"), seg("v"), jnp.pad(seg("dt"), ((0, 0), (0, 0), (0, DT_PAD - SSM_HEADS))), seg("z"), seg("ga"), seg("gb")],
        axis=-1).astype(BF16)
    wao, wso, wo = w_attn_out.astype(BF16), w_ssm_out.astype(BF16), w_o.astype(BF16)
    wup, wdn, wpg, wpp = w_up.astype(BF16), w_down.astype(BF16), w_ple_gate.astype(BF16), w_ple_proj.astype(BF16)
    dtb = _pad_lanes(dt_bias, LANES)
    alog = _pad_lanes(a_log, LANES)
    dsk_x = jnp.repeat(d_skip, SSM_HEAD_DIM, axis=-1)
    cos, sin = _rope_tables(seq, batch, dec_batch)

    x = jnp.concatenate([x_prompt.reshape(n_prompt, D_MODEL), x_sample.reshape(n_sample, D_MODEL)], axis=0)
    ple = jnp.concatenate([p_prompt.reshape(depth, n_prompt, PLE_DIM), p_sample.reshape(depth, n_sample, PLE_DIM)], axis=1)

    kv_p = [[] for _ in range(N_GROUPS)]
    kv_s = [[] for _ in range(N_GROUPS)]
    ssm_p, ssm_s, conv_p, conv_s = [], [], [], []
    for i in range(depth):
        qk = _inproj(x, norm1_w[i][None], w_qk[i], cos, sin, tm=tm, tn=768)
        rest = _inproj(x, norm1_w[i][None], w_rest[i], tm=tm, tn=1024)

        os, ls = [], []
        for g in range(N_GROUPS):
            o_p, l_p = _attn_prompt(qk, rest, g, batch=batch, seq=seq)
            o_s, l_s = _attn_sample(qk, rest, caches[g][i], g, n_prompt=n_prompt, dec_batch=dec_batch)
            os.append(jnp.concatenate([o_p, o_s], axis=0))
            ls.append(jnp.concatenate([l_p, l_s], axis=0))

        ssd_params = (conv_w[i], conv_b[i][None], dtb[i][None], alog[i][None], dsk_x[i][None], ssm_norm_w[i][None])
        y_p, h_p = _ssd_prompt(rest, ssd_params, batch=batch, seq=seq)
        cs = jnp.pad(state_conv[i], ((0, 0), (0, 1), (0, 0))).reshape(n_sample, XBC_DIM)
        y_s, h_s = _ssd_sample(rest, cs, state_ssm[i].reshape(dec_batch, SSM_INNER, SSM_STATE), ssd_params,
                               n_prompt=n_prompt, dec_batch=dec_batch)
        y_ssm = jnp.concatenate([y_p, y_s], axis=0)

        x = _merge(x, os, ls, y_ssm, rest, wao[i], wso[i], wo[i], tm=tm)
        x = _mlp(x, ple[i], norm2_w[i][None], wup[i], wdn[i], wpg[i], wpp[i], tm=256)

        qk_p = qk[:n_prompt].reshape(batch, seq, QK_W)
        rest_p = rest[:n_prompt].reshape(batch, seq, REST_W)
        qk_s = qk[n_prompt:].reshape(dec_batch, dec_seq, QK_W)
        rest_s = rest[n_prompt:].reshape(dec_batch, dec_seq, REST_W)
        for g in range(N_GROUPS):
            w = min(WINDOWS[g], seq)
            ks = slice(N_GROUPS * GROUP_W + g * GROUP_W, N_GROUPS * GROUP_W + (g + 1) * GROUP_W)
            vs = slice(REST_V + g * GROUP_W, REST_V + (g + 1) * GROUP_W)
            kvp = jnp.stack([qk_p[:, -w:, ks], rest_p[:, -w:, vs]], axis=2)
            kv_p[g].append(kvp.reshape(batch, w, 2, ATTN_HPG, HEAD_DIM))
            kvn = jnp.stack([qk_s[:, :, ks], rest_s[:, :, vs]], axis=2).reshape(dec_batch, dec_seq, 2, ATTN_HPG, HEAD_DIM)
            lwin = caches[g].shape[2]
            kv_s[g].append(jnp.concatenate([caches[g][i], kvn], axis=1)[:, -lwin:])
        conv_p.append(rest_p[:, -(CONV_WIDTH - 1):, :XBC_DIM])
        conv_s.append(jnp.concatenate([state_conv[i], rest_s[:, :, :XBC_DIM]], axis=1)[:, -(CONV_WIDTH - 1):])
        ssm_p.append(h_p.reshape(batch, SSM_HEADS, SSM_HEAD_DIM, SSM_STATE))
        ssm_s.append(h_s.reshape(dec_batch, SSM_HEADS, SSM_HEAD_DIM, SSM_STATE))

    y = _final_norm(x, final_norm_w[None], tm=tm)
    y_prompt = y[:n_prompt].reshape(batch, seq, D_MODEL)
    y_sample = y[n_prompt:].reshape(dec_batch, dec_seq, D_MODEL)
    return (y_prompt, y_sample,
            jnp.stack(kv_p[0]), jnp.stack(kv_p[1]), jnp.stack(kv_p[2]), jnp.stack(ssm_p), jnp.stack(conv_p),
            jnp.stack(kv_s[0]), jnp.stack(kv_s[1]), jnp.stack(kv_s[2]), jnp.stack(ssm_s), jnp.stack(conv_s))
```

```python
import functools

import jax
import jax.numpy as jnp
from jax import lax
from jax.experimental import pallas as pl
from jax.experimental.pallas import tpu as pltpu

F32 = jnp.float32
BF16 = jnp.bfloat16

D_MODEL = 1024
HEAD_DIM = 64
ATTN_HPG = 4
GROUP_W = ATTN_HPG * HEAD_DIM
WINDOWS = (128, 512, 2048)
DILATIONS = (1, 4, 16)
N_GROUPS = 3
KEYS_PER_QUERY_BLOCK = 128
ROPE_THETA = 10000.0
PAST_LEN = 2048
DEC_SEQ = 4

SSM_INNER = 2048
SSM_HEADS = 32
SSM_HEAD_DIM = 64
SSM_GROUPS = 4
SSM_HPG = SSM_HEADS // SSM_GROUPS
SSM_STATE = 128
SSM_GROUP_W = SSM_INNER // SSM_GROUPS
CONV_WIDTH = 4
XBC_DIM = SSM_INNER + 2 * SSM_GROUPS * SSM_STATE
SSD_CHUNK = 128
D_FF = 4 * D_MODEL
PLE_DIM = 256
NORM_EPS = 1e-6
NEG_BIG = -1e30

QK_W = 2 * N_GROUPS * GROUP_W
REST_XBC = 0
REST_V = XBC_DIM
REST_DT = REST_V + N_GROUPS * GROUP_W
REST_Z = 4096
REST_G = REST_Z + SSM_INNER
REST_W = REST_G + 2 * D_MODEL
DT_PAD = 256
LANES = 128
SUBLANES = 8
DECAY_ROWS = 16

VMEM_LIMIT = 56 * 1024 * 1024

TM_PROJ_CANDIDATES = (1536, 1024, 512)
TM_MERGE = 512
TM_MLP = 256
BT_ATTN_SAMPLE = 8
BT_SSD_SAMPLE = 4


def _cparams(*sem):
    return pltpu.CompilerParams(dimension_semantics=sem, vmem_limit_bytes=VMEM_LIMIT)


def _sigmoid(x):
    return 1.0 / (1.0 + jnp.exp(-x))


def _softplus(x):
    return jnp.maximum(x, 0.0) + jnp.log(1.0 + jnp.exp(-jnp.abs(x)))


def _split3(x):
    hi = x.astype(BF16).astype(F32)
    r1 = x - hi
    mid = r1.astype(BF16).astype(F32)
    lo = (r1 - mid).astype(BF16).astype(F32)
    return hi, mid, lo


def _dot_sel(sel_bf16, x):
    hi, mid, lo = _split3(x)
    out = jnp.dot(sel_bf16, hi.astype(BF16), preferred_element_type=F32)
    out = out + jnp.dot(sel_bf16, mid.astype(BF16), preferred_element_type=F32)
    return out + jnp.dot(sel_bf16, lo.astype(BF16), preferred_element_type=F32)


def _expand_heads(v, e3_bf16):
    hi, mid, lo = _split3(v)
    lane = lax.broadcasted_iota(jnp.int32, v.shape, 1)
    packed = jnp.where(lane < SSM_HEADS, hi,
                       jnp.where(lane < 2 * SSM_HEADS, pltpu.roll(mid, SSM_HEADS, 1),
                                 jnp.where(lane < 3 * SSM_HEADS, pltpu.roll(lo, 2 * SSM_HEADS, 1), 0.0)))
    return jnp.dot(packed.astype(BF16), e3_bf16, preferred_element_type=F32)


def _inproj_body(x_ref, nw_ref, w_ref, *rest, rope, tn):
    if rope:
        cos_ref, sin_ref, o_ref, u_ref = rest
    else:
        o_ref, u_ref = rest

    @pl.when(pl.program_id(1) == 0)
    def _():
        x = x_ref[...]
        ms = jnp.mean(x * x, axis=-1, keepdims=True)
        u_ref[...] = (x * lax.rsqrt(ms + NORM_EPS) * nw_ref[...]).astype(BF16)

    acc = jnp.dot(u_ref[...], w_ref[...], preferred_element_type=F32)
    if rope:
        cos = cos_ref[...]
        sin = sin_ref[...]
        lane = lax.broadcasted_iota(jnp.int32, cos.shape, 1)
        first_half = (lane % HEAD_DIM) < (HEAD_DIM // 2)
        for c in range(tn // LANES):
            t = acc[:, c * LANES:(c + 1) * LANES]
            rot = jnp.where(first_half, pltpu.roll(t, LANES - HEAD_DIM // 2, 1), pltpu.roll(t, HEAD_DIM // 2, 1))
            o_ref[:, c * LANES:(c + 1) * LANES] = t * cos + rot * sin
    else:
        o_ref[...] = acc


def _inproj(x, nw, w, layer, cos=None, sin=None, *, tm, tn):
    n, d = x.shape
    nout = w.shape[2]
    rope = cos is not None
    in_specs = [
        pl.BlockSpec((tm, d), lambda i, j: (i, 0)),
        pl.BlockSpec((None, 1, d), lambda i, j: (layer, 0, 0)),
        pl.BlockSpec((None, d, tn), lambda i, j: (layer, 0, j)),
    ]
    args = [x, nw, w]
    if rope:
        in_specs += [pl.BlockSpec((tm, LANES), lambda i, j: (i, 0)), pl.BlockSpec((tm, LANES), lambda i, j: (i, 0))]
        args += [cos, sin]
    return pl.pallas_call(
        functools.partial(_inproj_body, rope=rope, tn=tn),
        grid=(n // tm, nout // tn),
        in_specs=in_specs,
        out_specs=pl.BlockSpec((tm, tn), lambda i, j: (i, j)),
        out_shape=jax.ShapeDtypeStruct((n, nout), F32),
        scratch_shapes=[pltpu.VMEM((tm, d), BF16)],
        compiler_params=_cparams("parallel", "arbitrary"),
    )(*args)


def _attn_prompt_body(q_ref, kp_ref, kc_ref, vp_ref, vc_ref, o_ref, l_ref, *, d):
    blk = KEYS_PER_QUERY_BLOCK
    hw = q_ref.shape[1]
    n = pl.program_id(1)
    qi = lax.broadcasted_iota(jnp.int32, (blk, 2 * blk), 0)
    ki = lax.broadcasted_iota(jnp.int32, (blk, 2 * blk), 1)
    dist = blk + qi - ki
    mask = (dist >= 0) & (dist <= blk) & ((ki >= blk) | (n > 0))
    lane_head = lax.broadcasted_iota(jnp.int32, (blk, hw), 1) // HEAD_DIM

    def residue(r):
        rows = slice(None) if d == 1 else pl.ds(r, blk, stride=d)
        q = q_ref[rows, :] * (HEAD_DIM ** -0.5)
        kk = jnp.concatenate([kp_ref[rows, :], kc_ref[rows, :]], axis=0).astype(BF16)
        vv = jnp.concatenate([vp_ref[rows, :], vc_ref[rows, :]], axis=0).astype(BF16)
        o = jnp.zeros((blk, hw), F32)
        lse = jnp.zeros((blk, hw), F32)
        for h in range(hw // HEAD_DIM):
            hm = lane_head == h
            qh = jnp.where(hm, q, 0.0).astype(BF16)
            s = lax.dot_general(qh, kk, (((1,), (1,)), ((), ())), preferred_element_type=F32)
            s = jnp.where(mask, s, NEG_BIG)
            m = jnp.max(s, axis=-1, keepdims=True)
            p = jnp.exp(s - m)
            l = jnp.sum(p, axis=-1, keepdims=True)
            oh = jnp.dot(p.astype(BF16), vv, preferred_element_type=F32)
            o = jnp.where(hm, oh / l, o)
            lse = jnp.where(hm, m + jnp.log(l), lse)
        o_ref[rows, :] = o
        l_ref[rows, :] = lse

    if d == 1:
        residue(0)
    else:
        def step(r, carry):
            residue(r)
            return carry
        lax.fori_loop(0, d, step, 0)


def _attn_prompt(qk, rest, g, *, batch, seq):
    d = DILATIONS[g]
    span = d * KEYS_PER_QUERY_BLOCK
    nbt = seq // span
    hw = GROUP_W if d == 1 else LANES
    per = GROUP_W // hw
    q_col, k_col, v_col = g * per, (N_GROUPS + g) * per, (REST_V // GROUP_W + g) * per

    def cur(col):
        return lambda b, i, c: (b * nbt + i, col + c)

    def prev(col):
        return lambda b, i, c: (b * nbt + jnp.maximum(i - 1, 0), col + c)

    bs = (span, hw)
    out_sds = jax.ShapeDtypeStruct((batch * seq, GROUP_W), F32)
    return pl.pallas_call(
        functools.partial(_attn_prompt_body, d=d),
        grid=(batch, nbt, per),
        in_specs=[
            pl.BlockSpec(bs, cur(q_col)),
            pl.BlockSpec(bs, prev(k_col)),
            pl.BlockSpec(bs, cur(k_col)),
            pl.BlockSpec(bs, prev(v_col)),
            pl.BlockSpec(bs, cur(v_col)),
        ],
        out_specs=[pl.BlockSpec(bs, lambda b, i, c: (b * nbt + i, c))] * 2,
        out_shape=[out_sds, out_sds],
        compiler_params=_cparams("parallel", "arbitrary", "arbitrary"),
    )(qk, qk, qk, rest, rest)


def _attn_sample_body(q_ref, kn_ref, vn_ref, *rest, d, bt):
    n_res = 1 if d == 1 else DEC_SEQ
    cache_refs = rest[:n_res]
    o_ref, l_ref = rest[n_res:]
    nk = KEYS_PER_QUERY_BLOCK
    rows = DEC_SEQ * ATTN_HPG
    row_tok = lax.broadcasted_iota(jnp.int32, (rows, 1), 0) // ATTN_HPG
    row_head = lax.broadcasted_iota(jnp.int32, (rows, GROUP_W), 0) % ATTN_HPG
    lane_head = lax.broadcasted_iota(jnp.int32, (rows, GROUP_W), 1) // HEAD_DIM
    own = lane_head == row_head
    key_row = lax.broadcasted_iota(jnp.int32, (rows, nk), 1)
    cache_valid = key_row >= (row_tok // d)
    new_valid = [(row_tok >= jj) & (((row_tok - jj) % d) == 0) for jj in range(DEC_SEQ)]
    o_out, l_out = [], []
    for bi in range(bt):
        sl = slice(bi * DEC_SEQ, (bi + 1) * DEC_SEQ)
        qb = q_ref[sl, :] * (HEAD_DIM ** -0.5)
        knew = kn_ref[sl, :]
        vnew = vn_ref[sl, :]
        qrep = jnp.concatenate([jnp.broadcast_to(qb[j:j + 1, :], (ATTN_HPG, GROUP_W)) for j in range(DEC_SEQ)], axis=0)
        qrows = jnp.where(own, qrep, 0.0)
        qrows_b = qrows.astype(BF16)
        if d == 1:
            kc = cache_refs[0][bi, :, 0:GROUP_W].astype(BF16)
            s_c = lax.dot_general(qrows_b, kc, (((1,), (1,)), ((), ())), preferred_element_type=F32)
        else:
            s_c = jnp.zeros((rows, nk), F32)
            for j in range(DEC_SEQ):
                kc = cache_refs[j][bi, :, 0:GROUP_W].astype(BF16)
                sj = lax.dot_general(qrows_b, kc, (((1,), (1,)), ((), ())), preferred_element_type=F32)
                s_c = jnp.where(row_tok == j, sj, s_c)
        s_c = jnp.where(cache_valid, s_c, NEG_BIG)
        s_n = [jnp.where(new_valid[jj], jnp.sum(qrows * knew[jj:jj + 1, :], axis=-1, keepdims=True), NEG_BIG)
               for jj in range(DEC_SEQ)]
        m = jnp.max(s_c, axis=-1, keepdims=True)
        for jj in range(DEC_SEQ):
            m = jnp.maximum(m, s_n[jj])
        p_c = jnp.exp(s_c - m)
        l = jnp.sum(p_c, axis=-1, keepdims=True)
        p_cb = p_c.astype(BF16)
        if d == 1:
            vc = cache_refs[0][bi, :, GROUP_W:2 * GROUP_W].astype(BF16)
            acc = jnp.dot(p_cb, vc, preferred_element_type=F32)
        else:
            acc = jnp.zeros((rows, GROUP_W), F32)
            for j in range(DEC_SEQ):
                vc = cache_refs[j][bi, :, GROUP_W:2 * GROUP_W].astype(BF16)
                aj = jnp.dot(p_cb, vc, preferred_element_type=F32)
                acc = jnp.where(row_tok == j, aj, acc)
        for jj in range(DEC_SEQ):
            p_n = jnp.exp(s_n[jj] - m)
            l = l + p_n
            acc = acc + p_n * vnew[jj:jj + 1, :]
        o_rows = jnp.where(own, acc / l, 0.0)
        l_rows = jnp.where(own, m + jnp.log(l), 0.0)
        for j in range(DEC_SEQ):
            rs = slice(j * ATTN_HPG, (j + 1) * ATTN_HPG)
            o_out.append(jnp.sum(o_rows[rs, :], axis=0, keepdims=True))
            l_out.append(jnp.sum(l_rows[rs, :], axis=0, keepdims=True))
    o_ref[...] = jnp.concatenate(o_out, axis=0)
    l_ref[...] = jnp.concatenate(l_out, axis=0)


def _attn_sample(qk, rest, cache_v, g, layer, *, n_prompt, dec_batch):
    d = DILATIONS[g]
    bt = BT_ATTN_SAMPLE
    nk = KEYS_PER_QUERY_BLOCK
    n_res = 1 if d == 1 else DEC_SEQ
    rt = bt * DEC_SEQ
    row0 = n_prompt // rt
    rb = (rt, GROUP_W)
    out_sds = jax.ShapeDtypeStruct((dec_batch * DEC_SEQ, GROUP_W), F32)
    return pl.pallas_call(
        functools.partial(_attn_sample_body, d=d, bt=bt),
        grid=(dec_batch // bt,),
        in_specs=[
            pl.BlockSpec(rb, lambda i: (row0 + i, g)),
            pl.BlockSpec(rb, lambda i: (row0 + i, N_GROUPS + g)),
            pl.BlockSpec(rb, lambda i: (row0 + i, REST_V // GROUP_W + g)),
        ] + [pl.BlockSpec((None, bt, nk, 2 * GROUP_W), functools.partial(lambda i, j: (layer, i, 0, j), j=j))
             for j in range(n_res)],
        out_specs=[pl.BlockSpec(rb, lambda i: (i, 0))] * 2,
        out_shape=[out_sds, out_sds],
        compiler_params=_cparams("parallel"),
    )(qk, qk, rest, *([cache_v] * n_res))


def _gated_group_norm(y, z, snw):
    yg = y * (z * _sigmoid(z))
    outs = []
    for g in range(SSM_GROUPS):
        seg = yg[:, g * SSM_GROUP_W:(g + 1) * SSM_GROUP_W]
        ms = jnp.mean(seg * seg, axis=-1, keepdims=True)
        outs.append(seg * lax.rsqrt(ms + NORM_EPS))
    return (jnp.concatenate(outs, axis=-1) * snw).astype(BF16)


def _decay_rows(cd_row):
    hi, mid, lo = _split3(cd_row)
    zeros = jnp.zeros((DECAY_ROWS - 3, cd_row.shape[1]), F32)
    return jnp.concatenate([hi, mid, lo, zeros], axis=0).astype(BF16)


def _ssd_prompt_body(xbc_ref, z_ref, dt_ref, cw_ref, cb_ref, dtb_ref, alog_ref, dsk_ref, snw_ref, e_ref, tri_ref,
                     y_ref, h_ref, carry_ref):
    q = SSD_CHUNK
    c = pl.program_id(1)

    @pl.when(c == 0)
    def _():
        h_ref[...] = jnp.zeros_like(h_ref)
        carry_ref[...] = jnp.zeros_like(carry_ref)

    x = xbc_ref[...]
    top = lax.broadcasted_iota(jnp.int32, (SUBLANES, XBC_DIM), 0) == 0

    def shift_in(t, k):
        rolled = pltpu.roll(t, 1, 0)
        head = jnp.where(top, carry_ref[k], rolled[0:SUBLANES, :])
        carry_ref[k] = rolled[0:SUBLANES, :]
        return jnp.concatenate([head, rolled[SUBLANES:, :]], axis=0)

    t = x * cw_ref[0:1, :]
    for k in range(1, CONV_WIDTH):
        t = x * cw_ref[k:k + 1, :] + shift_in(t, k - 1)
    conv = cb_ref[...] + t
    act = conv * _sigmoid(conv)
    xs = act[:, :SSM_INNER]
    bm = act[:, SSM_INNER:SSM_INNER + SSM_GROUPS * SSM_STATE]
    cm = act[:, SSM_INNER + SSM_GROUPS * SSM_STATE:]

    e_mat = e_ref[...]
    dt = _softplus(dt_ref[:, 0:LANES] + dtb_ref[...])
    a = -jnp.exp(alog_ref[...])
    da = dt * a
    acum = _dot_sel(tri_ref[...], da)
    acum_t = acum.T
    alast = acum[q - 1:q, :]
    dt_x = _expand_heads(dt, e_mat)
    de_x = _expand_heads(jnp.exp(alast - acum), e_mat)
    ea_x = _expand_heads(jnp.exp(acum), e_mat)
    cd_x = _expand_heads(jnp.broadcast_to(jnp.exp(alast), (SUBLANES, LANES)), e_mat)[0:1, :]

    xdt = xs * dt_x
    xdt_b = xdt.astype(BF16)
    xsd_b = (xdt * de_x).astype(BF16)
    li = lax.broadcasted_iota(jnp.int32, (q, q), 0)
    si = lax.broadcasted_iota(jnp.int32, (q, q), 1)
    causal = li >= si
    y_parts, st_parts = [], []
    for g in range(SSM_GROUPS):
        bg = bm[:, g * SSM_STATE:(g + 1) * SSM_STATE].astype(BF16)
        cg = cm[:, g * SSM_STATE:(g + 1) * SSM_STATE].astype(BF16)
        cb = lax.dot_general(cg, bg, (((1,), (1,)), ((), ())), preferred_element_type=F32)
        hg = h_ref[0, g * SSM_GROUP_W:(g + 1) * SSM_GROUP_W, :].astype(BF16)
        y_off = lax.dot_general(cg, hg, (((1,), (1,)), ((), ())), preferred_element_type=F32)
        yd = []
        for e in range(SSM_HPG):
            he = g * SSM_HPG + e
            seg = acum[:, he:he + 1] - acum_t[he:he + 1, :]
            lm = jnp.exp(jnp.where(causal, seg, NEG_BIG))
            mm = (cb * lm).astype(BF16)
            yd.append(jnp.dot(mm, xdt_b[:, he * SSM_HEAD_DIM:(he + 1) * SSM_HEAD_DIM], preferred_element_type=F32))
        y_parts.append(jnp.concatenate(yd, axis=-1) + y_off * ea_x[:, g * SSM_GROUP_W:(g + 1) * SSM_GROUP_W])
        st_parts.append(lax.dot_general(xsd_b[:, g * SSM_GROUP_W:(g + 1) * SSM_GROUP_W], bg,
                                        (((0,), (0,)), ((), ())), preferred_element_type=F32))
    y = jnp.concatenate(y_parts, axis=-1) + dsk_ref[...] * xs
    dmat = lax.dot_general(_decay_rows(cd_x), jnp.ones((DECAY_ROWS, SSM_STATE), BF16), (((0,), (0,)), ((), ())),
                           preferred_element_type=F32)
    h_ref[0] = h_ref[0] * dmat + jnp.concatenate(st_parts, axis=0)
    y_ref[...] = _gated_group_norm(y, z_ref[...], snw_ref[...])


def _ssd_consts():
    rows = jnp.arange(LANES)[:, None]
    head = jnp.arange(SSM_INNER)[None, :] // SSM_HEAD_DIM
    e3 = ((rows % SSM_HEADS == head) & (rows < 3 * SSM_HEADS)).astype(BF16)
    tri = (jnp.arange(SSD_CHUNK)[:, None] >= jnp.arange(SSD_CHUNK)[None, :]).astype(BF16)
    return e3, tri


def _ssd_param_specs(layer):
    fixed = (lambda *_: (layer, 0, 0))
    return [
        pl.BlockSpec((None, CONV_WIDTH, XBC_DIM), fixed),
        pl.BlockSpec((None, 1, XBC_DIM), fixed),
        pl.BlockSpec((None, 1, LANES), fixed),
        pl.BlockSpec((None, 1, LANES), fixed),
        pl.BlockSpec((None, 1, SSM_INNER), fixed),
        pl.BlockSpec((None, 1, SSM_INNER), fixed),
        pl.BlockSpec((LANES, SSM_INNER), lambda *_: (0, 0)),
    ]


def _ssd_prompt(rest, params, layer, *, batch, seq):
    q = SSD_CHUNK
    nc = seq // q
    e3, tri = _ssd_consts()
    row = lambda b, c: b * nc + c
    return pl.pallas_call(
        _ssd_prompt_body,
        grid=(batch, nc),
        in_specs=[
            pl.BlockSpec((q, XBC_DIM), lambda b, c: (row(b, c), REST_XBC // XBC_DIM)),
            pl.BlockSpec((q, SSM_INNER), lambda b, c: (row(b, c), REST_Z // SSM_INNER)),
            pl.BlockSpec((q, DT_PAD), lambda b, c: (row(b, c), REST_DT // DT_PAD)),
        ] + _ssd_param_specs(layer) + [pl.BlockSpec((q, q), lambda b, c: (0, 0))],
        out_specs=[
            pl.BlockSpec((q, SSM_INNER), lambda b, c: (row(b, c), 0)),
            pl.BlockSpec((1, SSM_INNER, SSM_STATE), lambda b, c: (b, 0, 0)),
        ],
        out_shape=[
            jax.ShapeDtypeStruct((batch * seq, SSM_INNER), BF16),
            jax.ShapeDtypeStruct((batch, SSM_INNER, SSM_STATE), F32),
        ],
        scratch_shapes=[pltpu.VMEM((CONV_WIDTH - 1, SUBLANES, XBC_DIM), F32)],
        compiler_params=_cparams("parallel", "arbitrary"),
    )(rest, rest, rest, *params, e3, tri)


def _ssd_sample_body(xbc_ref, z_ref, dt_ref, cs_ref, h0_ref, cw_ref, cb_ref, dtb_ref, alog_ref, dsk_ref, snw_ref,
                     e_ref, y_ref, h_ref, *, bt):
    r = bt * DEC_SEQ
    lrow = lax.broadcasted_iota(jnp.int32, (r, 1), 0) % DEC_SEQ
    rowb = lax.broadcasted_iota(jnp.int32, (r, 1), 0) // DEC_SEQ

    def shift(v, k):
        return v if k == 0 else pltpu.roll(v, k, 0)

    x = xbc_ref[...]
    cs = cs_ref[...]
    conv = cb_ref[...]
    for j in range(CONV_WIDTH):
        k = CONV_WIDTH - 1 - j
        if k == 0:
            term = x
        else:
            carried = cs if k == CONV_WIDTH - 1 else pltpu.roll(cs, r - (CONV_WIDTH - 1 - k), 0)
            term = jnp.where(lrow >= k, shift(x, k), carried)
        conv = conv + term * cw_ref[j:j + 1, :]
    act = conv * _sigmoid(conv)
    xs = act[:, :SSM_INNER]
    bm = act[:, SSM_INNER:SSM_INNER + SSM_GROUPS * SSM_STATE]
    cm = act[:, SSM_INNER + SSM_GROUPS * SSM_STATE:]

    e_mat = e_ref[...]
    dt = _softplus(dt_ref[:, 0:LANES] + dtb_ref[...])
    a = -jnp.exp(alog_ref[...])
    da = dt * a
    acum = da
    for k in range(1, DEC_SEQ):
        acum = acum + jnp.where(lrow >= k, shift(da, k), 0.0)
    alast = acum
    for k in range(1, DEC_SEQ):
        alast = jnp.where(lrow == DEC_SEQ - 1 - k, pltpu.roll(acum, r - k, 0), alast)

    lane_grp = lax.broadcasted_iota(jnp.int32, (r, LANES), 1) // SSM_HPG
    g_rows = []
    for k in range(DEC_SEQ):
        bk = shift(bm, k)
        cbk = jnp.zeros((r, LANES), F32)
        for g in range(SSM_GROUPS):
            sl = slice(g * SSM_STATE, (g + 1) * SSM_STATE)
            cbk = jnp.where(lane_grp == g, jnp.sum(cm[:, sl] * bk[:, sl], axis=-1, keepdims=True), cbk)
        g_rows.append(jnp.where(lrow >= k, cbk * jnp.exp(acum - shift(acum, k)), 0.0))
    stacked = jnp.concatenate([dt, jnp.exp(alast - acum), jnp.exp(acum), jnp.exp(alast)] + g_rows, axis=0)
    ex = _expand_heads(stacked, e_mat)
    dt_x, de_x, ea_x, cd_x = ex[0:r], ex[r:2 * r], ex[2 * r:3 * r], ex[3 * r:4 * r]

    xdt = xs * dt_x
    y = dsk_ref[...] * xs
    for k in range(DEC_SEQ):
        y = y + ex[(4 + k) * r:(5 + k) * r] * shift(xdt, k)

    xsd = xdt * de_x
    ones = jnp.ones((DECAY_ROWS, SSM_STATE), BF16)
    y_off = jnp.zeros((r, SSM_INNER), F32)
    for bi in range(bt):
        mine = rowb == bi
        xsd_b = jnp.where(mine, xsd, 0.0).astype(BF16)
        h0 = h0_ref[bi]
        yo, st = [], []
        for g in range(SSM_GROUPS):
            gs = slice(g * SSM_GROUP_W, (g + 1) * SSM_GROUP_W)
            bg = bm[:, g * SSM_STATE:(g + 1) * SSM_STATE].astype(BF16)
            cg = cm[:, g * SSM_STATE:(g + 1) * SSM_STATE].astype(BF16)
            yo.append(lax.dot_general(cg, h0[gs, :].astype(BF16), (((1,), (1,)), ((), ())), preferred_element_type=F32))
            st.append(lax.dot_general(xsd_b[:, gs], bg, (((0,), (0,)), ((), ())), preferred_element_type=F32))
        y_off = jnp.where(mine, jnp.concatenate(yo, axis=-1), y_off)
        dmat = lax.dot_general(_decay_rows(cd_x[bi * DEC_SEQ:bi * DEC_SEQ + 1, :]), ones, (((0,), (0,)), ((), ())),
                               preferred_element_type=F32)
        h_ref[bi] = h0 * dmat + jnp.concatenate(st, axis=0)
    y = y + y_off * ea_x
    y_ref[...] = _gated_group_norm(y, z_ref[...], snw_ref[...])


def _ssd_sample(rest, conv_state, h0, params, layer, *, n_prompt, dec_batch):
    bt = BT_SSD_SAMPLE
    r = bt * DEC_SEQ
    row0 = n_prompt // r
    e3, _ = _ssd_consts()
    return pl.pallas_call(
        functools.partial(_ssd_sample_body, bt=bt),
        grid=(dec_batch // bt,),
        in_specs=[
            pl.BlockSpec((r, XBC_DIM), lambda i: (row0 + i, REST_XBC // XBC_DIM)),
            pl.BlockSpec((r, SSM_INNER), lambda i: (row0 + i, REST_Z // SSM_INNER)),
            pl.BlockSpec((r, DT_PAD), lambda i: (row0 + i, REST_DT // DT_PAD)),
            pl.BlockSpec((None, r, XBC_DIM), lambda i: (layer, i, 0)),
            pl.BlockSpec((None, bt, SSM_INNER, SSM_STATE), lambda i: (layer, i, 0, 0)),
        ] + _ssd_param_specs(layer),
        out_specs=[
            pl.BlockSpec((r, SSM_INNER), lambda i: (i, 0)),
            pl.BlockSpec((bt, SSM_INNER, SSM_STATE), lambda i: (i, 0, 0)),
        ],
        out_shape=[
            jax.ShapeDtypeStruct((dec_batch * DEC_SEQ, SSM_INNER), BF16),
            jax.ShapeDtypeStruct((dec_batch, SSM_INNER, SSM_STATE), F32),
        ],
        compiler_params=_cparams("parallel"),
    )(rest, rest, rest, conv_state, h0, *params, e3)


def _merge_body(x_ref, *refs, prompt_tiles):
    (op0, os0, op1, os1, op2, os2, lp0, ls0, lp1, ls1, lp2, ls2, yp_ref, ys_ref, g_ref,
     wao_ref, wso_ref, wo_ref, out_ref) = refs
    is_sample = pl.program_id(0) >= prompt_tiles

    def pick(p_ref, s_ref):
        return jnp.where(is_sample, s_ref[...], p_ref[...])

    l0, l1, l2 = pick(lp0, ls0), pick(lp1, ls1), pick(lp2, ls2)
    m = jnp.maximum(jnp.maximum(l0, l1), l2)
    e0, e1, e2 = jnp.exp(l0 - m), jnp.exp(l1 - m), jnp.exp(l2 - m)
    o_attn = (e0 * pick(op0, os0) + e1 * pick(op1, os1) + e2 * pick(op2, os2)) / (e0 + e1 + e2)
    att = jnp.dot(o_attn.astype(BF16), wao_ref[...], preferred_element_type=F32)
    ssm = jnp.dot(pick(yp_ref, ys_ref), wso_ref[...], preferred_element_type=F32)
    merged = _sigmoid(g_ref[:, :D_MODEL]) * att + _sigmoid(g_ref[:, D_MODEL:]) * ssm
    out_ref[...] = x_ref[...] + jnp.dot(merged.astype(BF16), wo_ref[...], preferred_element_type=F32)


def _merge(x, o_p, o_s, l_p, l_s, y_p, y_s, rest, wao, wso, wo, layer, *, tm):
    n = x.shape[0]
    n_prompt = y_p.shape[0]
    assert n_prompt % tm == 0 and y_s.shape[0] % tm == 0
    pt = n_prompt // tm
    row = lambda i: (i, 0)
    prow = lambda i: (jnp.minimum(i, pt - 1), 0)
    srow = lambda i: (jnp.maximum(i - pt, 0), 0)
    wfix = lambda i: (layer, 0, 0)

    def pair(width):
        return [pl.BlockSpec((tm, width), prow), pl.BlockSpec((tm, width), srow)]

    args = [x]
    for p, s in zip(o_p + l_p, o_s + l_s):
        args += [p, s]
    args += [y_p, y_s, rest, wao, wso, wo]
    return pl.pallas_call(
        functools.partial(_merge_body, prompt_tiles=pt),
        grid=(n // tm,),
        in_specs=[pl.BlockSpec((tm, D_MODEL), row)] + pair(GROUP_W) * 6 + pair(SSM_INNER) + [
            pl.BlockSpec((tm, 2 * D_MODEL), lambda i: (i, REST_G // (2 * D_MODEL))),
            pl.BlockSpec((None,) + wao.shape[1:], wfix),
            pl.BlockSpec((None,) + wso.shape[1:], wfix),
            pl.BlockSpec((None,) + wo.shape[1:], wfix),
        ],
        out_specs=pl.BlockSpec((tm, D_MODEL), row),
        out_shape=jax.ShapeDtypeStruct((n, D_MODEL), F32),
        compiler_params=_cparams("parallel"),
    )(*args)


def _mlp_body(x_ref, ple_ref, n2_ref, wup_ref, wdn_ref, wpg_ref, wpp_ref, out_ref):
    x = x_ref[...]
    ms = jnp.mean(x * x, axis=-1, keepdims=True)
    u = (x * lax.rsqrt(ms + NORM_EPS) * n2_ref[...]).astype(BF16)
    hdn = jnp.maximum(jnp.dot(u, wup_ref[...], preferred_element_type=F32), 0.0)
    hdn = (hdn * hdn).astype(BF16)
    x = x + jnp.dot(hdn, wdn_ref[...], preferred_element_type=F32)
    gate = _sigmoid(jnp.dot(x.astype(BF16), wpg_ref[...], preferred_element_type=F32))
    out_ref[...] = x + gate * jnp.dot(ple_ref[...].astype(BF16), wpp_ref[...], preferred_element_type=F32)


def _mlp(x, ple, n2, wup, wdn, wpg, wpp, layer, *, tm):
    n = x.shape[0]
    row = lambda i: (i, 0)
    wfix = lambda i: (layer, 0, 0)
    single = pl.Buffered(1)

    def wspec(w):
        return pl.BlockSpec((None,) + w.shape[1:], wfix, pipeline_mode=single)

    return pl.pallas_call(
        _mlp_body,
        grid=(n // tm,),
        in_specs=[
            pl.BlockSpec((tm, D_MODEL), row),
            pl.BlockSpec((None, tm, PLE_DIM), lambda i: (layer, i, 0)),
            pl.BlockSpec((None, 1, D_MODEL), wfix),
            wspec(wup), wspec(wdn), wspec(wpg), wspec(wpp),
        ],
        out_specs=pl.BlockSpec((tm, D_MODEL), row),
        out_shape=jax.ShapeDtypeStruct((n, D_MODEL), F32),
        compiler_params=_cparams("parallel"),
    )(x, ple, n2, wup, wdn, wpg, wpp)


def _final_norm_body(x_ref, w_ref, o_ref):
    x = x_ref[...]
    ms = jnp.mean(x * x, axis=-1, keepdims=True)
    o_ref[...] = x * lax.rsqrt(ms + NORM_EPS) * w_ref[...]


def _final_norm(x, w, *, tm):
    n = x.shape[0]
    return pl.pallas_call(
        _final_norm_body,
        grid=(n // tm,),
        in_specs=[pl.BlockSpec((tm, D_MODEL), lambda i: (i, 0)), pl.BlockSpec((1, D_MODEL), lambda i: (0, 0))],
        out_specs=pl.BlockSpec((tm, D_MODEL), lambda i: (i, 0)),
        out_shape=jax.ShapeDtypeStruct((n, D_MODEL), F32),
        compiler_params=_cparams("parallel"),
    )(x, w)


def _rope_tables(seq, batch, dec_batch):
    half = HEAD_DIM // 2
    inv_freq = ROPE_THETA ** (-jnp.arange(half, dtype=F32) / half)
    pos = jnp.concatenate([jnp.tile(jnp.arange(seq), batch), jnp.tile(PAST_LEN + jnp.arange(DEC_SEQ), dec_batch)])
    ang = pos.astype(F32)[:, None] * inv_freq[None, :]
    cos, sin = jnp.cos(ang), jnp.sin(ang)
    reps = LANES // HEAD_DIM
    return jnp.tile(jnp.concatenate([cos, cos], axis=1), (1, reps)), jnp.tile(jnp.concatenate([-sin, sin], axis=1), (1, reps))


def _strided_cache_view(cache, g):
    depth, b, w = cache.shape[:3]
    d = DILATIONS[g]
    assert w == WINDOWS[g] and (d == 1 or d >= DEC_SEQ)
    n_res = 1 if d == 1 else DEC_SEQ
    v = cache.reshape(depth, b, w // d, d, 2 * GROUP_W)[:, :, :, :n_res]
    return v.reshape(depth, b, w // d, n_res * 2 * GROUP_W)


def kernel(x_prompt, x_sample, cache_kv_w128, cache_kv_w512, cache_kv_w2048, state_ssm, state_conv, p_prompt, p_sample, norm1_w, w_in, conv_w, conv_b, dt_bias, a_log, d_skip, ssm_norm_w, w_attn_out, w_ssm_out, w_o, norm2_w, w_up, w_down, w_ple_proj, w_ple_gate, final_norm_w):
    batch, seq, _ = x_prompt.shape
    dec_batch, dec_seq, _ = x_sample.shape
    depth = w_in.shape[0]
    assert dec_seq == DEC_SEQ and seq % (DILATIONS[-1] * KEYS_PER_QUERY_BLOCK) == 0 and seq >= WINDOWS[-1]
    n_prompt = batch * seq
    n_sample = dec_batch * dec_seq
    caches = (cache_kv_w128, cache_kv_w512, cache_kv_w2048)

    c0 = 0
    cols = {}
    for name, width in (("q", 768), ("k", 768), ("v", 768), ("z", SSM_INNER), ("xbc", XBC_DIM), ("dt", SSM_HEADS),
                        ("ga", D_MODEL), ("gb", D_MODEL)):
        cols[name] = (c0, c0 + width)
        c0 += width
    seg = lambda name: w_in[:, :, cols[name][0]:cols[name][1]]
    w_qk = w_in[:, :, :QK_W].astype(BF16)
    w_rest = jnp.concatenate(
        [seg("xbc"), seg("v"), jnp.pad(seg("dt"), ((0, 0), (0, 0), (0, DT_PAD - SSM_HEADS))), seg("z"), seg("ga"), seg("gb")],
        axis=-1).astype(BF16)
    wao, wso, wo = w_attn_out.astype(BF16), w_ssm_out.astype(BF16), w_o.astype(BF16)
    wup, wdn, wpg, wpp = w_up.astype(BF16), w_down.astype(BF16), w_ple_gate.astype(BF16), w_ple_proj.astype(BF16)
    n1 = norm1_w[:, None, :]
    n2 = norm2_w[:, None, :]
    pad_heads = lambda v: jnp.pad(v, ((0, 0), (0, LANES - SSM_HEADS)))[:, None, :]
    ssd_params = (conv_w, conv_b[:, None, :], pad_heads(dt_bias), pad_heads(a_log),
                  jnp.repeat(d_skip, SSM_HEAD_DIM, axis=-1)[:, None, :], ssm_norm_w[:, None, :])
    cos, sin = _rope_tables(seq, batch, dec_batch)
    cache_views = [_strided_cache_view(c, g) for g, c in enumerate(caches)]
    conv_carry = jnp.pad(state_conv, ((0, 0), (0, 0), (0, DEC_SEQ - (CONV_WIDTH - 1)), (0, 0))).reshape(depth, n_sample, XBC_DIM)
    h0_all = state_ssm.reshape(depth, dec_batch, SSM_INNER, SSM_STATE)

    x = jnp.concatenate([x_prompt.reshape(n_prompt, D_MODEL), x_sample.reshape(n_sample, D_MODEL)], axis=0)
    ple = jnp.concatenate([p_prompt.reshape(depth, n_prompt, PLE_DIM), p_sample.reshape(depth, n_sample, PLE_DIM)], axis=1)

    tm_proj = next(t for t in TM_PROJ_CANDIDATES if (n_prompt + n_sample) % t == 0)
    kv_p = [[] for _ in range(N_GROUPS)]
    kv_new = [[] for _ in range(N_GROUPS)]
    ssm_p, ssm_s, conv_p, conv_s = [], [], [], []
    for i in range(depth):
        qk = _inproj(x, n1, w_qk, i, cos, sin, tm=tm_proj, tn=QK_W // 2)
        rest = _inproj(x, n1, w_rest, i, tm=tm_proj, tn=1024)

        o_p, l_p, o_s, l_s = [], [], [], []
        for g in range(N_GROUPS):
            o, l = _attn_prompt(qk, rest, g, batch=batch, seq=seq)
            o_p.append(o)
            l_p.append(l)
            o, l = _attn_sample(qk, rest, cache_views[g], g, i, n_prompt=n_prompt, dec_batch=dec_batch)
            o_s.append(o)
            l_s.append(l)

        y_p, h_p = _ssd_prompt(rest, ssd_params, i, batch=batch, seq=seq)
        y_s, h_s = _ssd_sample(rest, conv_carry, h0_all, ssd_params, i, n_prompt=n_prompt, dec_batch=dec_batch)

        x = _merge(x, o_p, o_s, l_p, l_s, y_p, y_s, rest, wao, wso, wo, i, tm=TM_MERGE)
        x = _mlp(x, ple, n2, wup, wdn, wpg, wpp, i, tm=TM_MLP)

        for g in range(N_GROUPS):
            w = WINDOWS[g]
            ks = slice(N_GROUPS * GROUP_W + g * GROUP_W, N_GROUPS * GROUP_W + (g + 1) * GROUP_W)
            vs = slice(REST_V + g * GROUP_W, REST_V + (g + 1) * GROUP_W)
            k_tail = jnp.stack([qk[b * seq + seq - w:(b + 1) * seq, ks] for b in range(batch)])
            v_tail = jnp.stack([rest[b * seq + seq - w:(b + 1) * seq, vs] for b in range(batch)])
            kv_p[g].append(jnp.stack([k_tail, v_tail], axis=2).reshape(batch, w, 2, ATTN_HPG, HEAD_DIM))
            kvn = jnp.stack([qk[n_prompt:, ks], rest[n_prompt:, vs]], axis=1)
            kv_new[g].append(kvn.reshape(dec_batch, dec_seq, 2, ATTN_HPG, HEAD_DIM))
        tail = CONV_WIDTH - 1
        conv_p.append(jnp.stack([rest[(b + 1) * seq - tail:(b + 1) * seq, :XBC_DIM] for b in range(batch)]))
        xbc_s = rest[n_prompt:, :XBC_DIM].reshape(dec_batch, dec_seq, XBC_DIM)
        conv_s.append(jnp.concatenate([state_conv[i], xbc_s], axis=1)[:, -tail:])
        ssm_p.append(h_p.reshape(batch, SSM_HEADS, SSM_HEAD_DIM, SSM_STATE))
        ssm_s.append(h_s.reshape(dec_batch, SSM_HEADS, SSM_HEAD_DIM, SSM_STATE))

    y = _final_norm(x, final_norm_w[None], tm=TM_MERGE)
    y_prompt = y[:n_prompt].reshape(batch, seq, D_MODEL)
    y_sample = y[n_prompt:].reshape(dec_batch, dec_seq, D_MODEL)
    kv_s = [jnp.concatenate([caches[g][:, :, dec_seq:], jnp.stack(kv_new[g])], axis=2) for g in range(N_GROUPS)]
    return (y_prompt, y_sample,
            jnp.stack(kv_p[0]), jnp.stack(kv_p[1]), jnp.stack(kv_p[2]), jnp.stack(ssm_p), jnp.stack(conv_p),
            kv_s[0], kv_s[1], kv_s[2], jnp.stack(ssm_s), jnp.stack(conv_s))
```

```python
import functools

import jax
import jax.numpy as jnp
from jax import lax
from jax.experimental import pallas as pl
from jax.experimental.pallas import tpu as pltpu

F32 = jnp.float32
BF16 = jnp.bfloat16

D_MODEL = 1024
HEAD_DIM = 64
ATTN_HPG = 4
GROUP_W = ATTN_HPG * HEAD_DIM
WINDOWS = (128, 512, 2048)
DILATIONS = (1, 4, 16)
N_GROUPS = 3
KEYS_PER_QUERY_BLOCK = 128
ROPE_THETA = 10000.0
PAST_LEN = 2048
DEC_SEQ = 4

SSM_INNER = 2048
SSM_HEADS = 32
SSM_HEAD_DIM = 64
SSM_GROUPS = 4
SSM_HPG = SSM_HEADS // SSM_GROUPS
SSM_STATE = 128
SSM_GROUP_W = SSM_INNER // SSM_GROUPS
CONV_WIDTH = 4
XBC_DIM = SSM_INNER + 2 * SSM_GROUPS * SSM_STATE
SSD_CHUNK = 128
D_FF = 4 * D_MODEL
PLE_DIM = 256
NORM_EPS = 1e-6
NEG_BIG = -1e30

QK_W = 2 * N_GROUPS * GROUP_W
REST_XBC = 0
REST_V = XBC_DIM
REST_DT = REST_V + N_GROUPS * GROUP_W
REST_Z = 4096
REST_G = REST_Z + SSM_INNER
REST_W = REST_G + 2 * D_MODEL
DT_PAD = 256
LANES = 128
SUBLANES = 8
DECAY_ROWS = 16

VMEM_LIMIT = 56 * 1024 * 1024

TM_PROJ_CANDIDATES = (1536, 1024, 512)
TM_MERGE = 512
TM_MLP = 256
BT_ATTN_SAMPLE = 8
BT_SSD_SAMPLE = 4
ROLL_BATCH_CHUNK = 32


def _cparams(*sem):
    return pltpu.CompilerParams(dimension_semantics=sem, vmem_limit_bytes=VMEM_LIMIT)


def _sigmoid(x):
    return 1.0 / (1.0 + jnp.exp(-x))


def _softplus(x):
    return jnp.maximum(x, 0.0) + jnp.log(1.0 + jnp.exp(-jnp.abs(x)))


def _split3(x):
    hi = x.astype(BF16).astype(F32)
    r1 = x - hi
    mid = r1.astype(BF16).astype(F32)
    lo = (r1 - mid).astype(BF16).astype(F32)
    return hi, mid, lo


def _dot_sel(sel_bf16, x):
    hi, mid, lo = _split3(x)
    out = jnp.dot(sel_bf16, hi.astype(BF16), preferred_element_type=F32)
    out = out + jnp.dot(sel_bf16, mid.astype(BF16), preferred_element_type=F32)
    return out + jnp.dot(sel_bf16, lo.astype(BF16), preferred_element_type=F32)


def _expand_heads(v, e3_bf16):
    hi, mid, lo = _split3(v)
    lane = lax.broadcasted_iota(jnp.int32, v.shape, 1)
    packed = jnp.where(lane < SSM_HEADS, hi,
                       jnp.where(lane < 2 * SSM_HEADS, pltpu.roll(mid, SSM_HEADS, 1),
                                 jnp.where(lane < 3 * SSM_HEADS, pltpu.roll(lo, 2 * SSM_HEADS, 1), 0.0)))
    return jnp.dot(packed.astype(BF16), e3_bf16, preferred_element_type=F32)


def _inproj_body(x_ref, nw_ref, w_ref, *rest, rope, tn):
    if rope:
        cos_ref, sin_ref, o_ref, u_ref = rest
    else:
        o_ref, u_ref = rest

    @pl.when(pl.program_id(1) == 0)
    def _():
        x = x_ref[...]
        ms = jnp.mean(x * x, axis=-1, keepdims=True)
        u_ref[...] = (x * lax.rsqrt(ms + NORM_EPS) * nw_ref[...]).astype(BF16)

    acc = jnp.dot(u_ref[...], w_ref[...], preferred_element_type=F32)
    if rope:
        cos = cos_ref[...]
        sin = sin_ref[...]
        lane = lax.broadcasted_iota(jnp.int32, cos.shape, 1)
        first_half = (lane % HEAD_DIM) < (HEAD_DIM // 2)
        for c in range(tn // LANES):
            t = acc[:, c * LANES:(c + 1) * LANES]
            rot = jnp.where(first_half, pltpu.roll(t, LANES - HEAD_DIM // 2, 1), pltpu.roll(t, HEAD_DIM // 2, 1))
            o_ref[:, c * LANES:(c + 1) * LANES] = t * cos + rot * sin
    else:
        o_ref[...] = acc


def _inproj(x, nw, w, layer, cos=None, sin=None, *, tm, tn):
    n, d = x.shape
    nout = w.shape[2]
    rope = cos is not None
    in_specs = [
        pl.BlockSpec((tm, d), lambda i, j: (i, 0)),
        pl.BlockSpec((None, 1, d), lambda i, j: (layer, 0, 0)),
        pl.BlockSpec((None, d, tn), lambda i, j: (layer, 0, j)),
    ]
    args = [x, nw, w]
    if rope:
        in_specs += [pl.BlockSpec((tm, LANES), lambda i, j: (i, 0)), pl.BlockSpec((tm, LANES), lambda i, j: (i, 0))]
        args += [cos, sin]
    return pl.pallas_call(
        functools.partial(_inproj_body, rope=rope, tn=tn),
        grid=(n // tm, nout // tn),
        in_specs=in_specs,
        out_specs=pl.BlockSpec((tm, tn), lambda i, j: (i, j)),
        out_shape=jax.ShapeDtypeStruct((n, nout), F32),
        scratch_shapes=[pltpu.VMEM((tm, d), BF16)],
        compiler_params=_cparams("parallel", "arbitrary"),
    )(*args)


def _attn_prompt_body(q_ref, kp_ref, kc_ref, vp_ref, vc_ref, o_ref, l_ref, *, d):
    blk = KEYS_PER_QUERY_BLOCK
    hw = q_ref.shape[1]
    n = pl.program_id(1)
    qi = lax.broadcasted_iota(jnp.int32, (blk, 2 * blk), 0)
    ki = lax.broadcasted_iota(jnp.int32, (blk, 2 * blk), 1)
    dist = blk + qi - ki
    mask = (dist >= 0) & (dist <= blk) & ((ki >= blk) | (n > 0))
    lane_head = lax.broadcasted_iota(jnp.int32, (blk, hw), 1) // HEAD_DIM

    def residue(r):
        rows = slice(None) if d == 1 else pl.ds(r, blk, stride=d)
        q = q_ref[rows, :] * (HEAD_DIM ** -0.5)
        kk = jnp.concatenate([kp_ref[rows, :], kc_ref[rows, :]], axis=0).astype(BF16)
        vv = jnp.concatenate([vp_ref[rows, :], vc_ref[rows, :]], axis=0).astype(BF16)
        o = jnp.zeros((blk, hw), F32)
        lse = jnp.zeros((blk, hw), F32)
        for h in range(hw // HEAD_DIM):
            hm = lane_head == h
            qh = jnp.where(hm, q, 0.0).astype(BF16)
            s = lax.dot_general(qh, kk, (((1,), (1,)), ((), ())), preferred_element_type=F32)
            s = jnp.where(mask, s, NEG_BIG)
            m = jnp.max(s, axis=-1, keepdims=True)
            p = jnp.exp(s - m)
            l = jnp.sum(p, axis=-1, keepdims=True)
            oh = jnp.dot(p.astype(BF16), vv, preferred_element_type=F32)
            o = jnp.where(hm, oh / l, o)
            lse = jnp.where(hm, m + jnp.log(l), lse)
        o_ref[rows, :] = o
        l_ref[rows, :] = lse

    if d == 1:
        residue(0)
    else:
        def step(r, carry):
            residue(r)
            return carry
        lax.fori_loop(0, d, step, 0)


def _attn_prompt(qk, rest, g, *, batch, seq):
    d = DILATIONS[g]
    span = d * KEYS_PER_QUERY_BLOCK
    nbt = seq // span
    hw = GROUP_W if d == 1 else LANES
    per = GROUP_W // hw
    q_col, k_col, v_col = g * per, (N_GROUPS + g) * per, (REST_V // GROUP_W + g) * per

    def cur(col):
        return lambda b, i, c: (b * nbt + i, col + c)

    def prev(col):
        return lambda b, i, c: (b * nbt + jnp.maximum(i - 1, 0), col + c)

    bs = (span, hw)
    out_sds = jax.ShapeDtypeStruct((batch * seq, GROUP_W), F32)
    return pl.pallas_call(
        functools.partial(_attn_prompt_body, d=d),
        grid=(batch, nbt, per),
        in_specs=[
            pl.BlockSpec(bs, cur(q_col)),
            pl.BlockSpec(bs, prev(k_col)),
            pl.BlockSpec(bs, cur(k_col)),
            pl.BlockSpec(bs, prev(v_col)),
            pl.BlockSpec(bs, cur(v_col)),
        ],
        out_specs=[pl.BlockSpec(bs, lambda b, i, c: (b * nbt + i, c))] * 2,
        out_shape=[out_sds, out_sds],
        compiler_params=_cparams("parallel", "arbitrary", "arbitrary"),
    )(qk, qk, qk, rest, rest)


def _attn_sample_body(q_ref, kn_ref, vn_ref, *rest, d, bt):
    n_res = 1 if d == 1 else DEC_SEQ
    cache_refs = rest[:n_res]
    o_ref, l_ref = rest[n_res:]
    nk = KEYS_PER_QUERY_BLOCK
    rows = DEC_SEQ * ATTN_HPG
    row_tok = lax.broadcasted_iota(jnp.int32, (rows, 1), 0) // ATTN_HPG
    row_head = lax.broadcasted_iota(jnp.int32, (rows, GROUP_W), 0) % ATTN_HPG
    lane_head = lax.broadcasted_iota(jnp.int32, (rows, GROUP_W), 1) // HEAD_DIM
    own = lane_head == row_head
    key_row = lax.broadcasted_iota(jnp.int32, (rows, nk), 1)
    cache_valid = key_row >= (row_tok // d)
    new_valid = [(row_tok >= jj) & (((row_tok - jj) % d) == 0) for jj in range(DEC_SEQ)]
    o_out, l_out = [], []
    for bi in range(bt):
        sl = slice(bi * DEC_SEQ, (bi + 1) * DEC_SEQ)
        qb = q_ref[sl, :] * (HEAD_DIM ** -0.5)
        knew = kn_ref[sl, :]
        vnew = vn_ref[sl, :]
        qrep = jnp.concatenate([jnp.broadcast_to(qb[j:j + 1, :], (ATTN_HPG, GROUP_W)) for j in range(DEC_SEQ)], axis=0)
        qrows = jnp.where(own, qrep, 0.0)
        qrows_b = qrows.astype(BF16)
        if d == 1:
            kc = cache_refs[0][bi, :, 0:GROUP_W].astype(BF16)
            s_c = lax.dot_general(qrows_b, kc, (((1,), (1,)), ((), ())), preferred_element_type=F32)
        else:
            s_c = jnp.zeros((rows, nk), F32)
            for j in range(DEC_SEQ):
                kc = cache_refs[j][bi, :, 0:GROUP_W].astype(BF16)
                sj = lax.dot_general(qrows_b, kc, (((1,), (1,)), ((), ())), preferred_element_type=F32)
                s_c = jnp.where(row_tok == j, sj, s_c)
        s_c = jnp.where(cache_valid, s_c, NEG_BIG)
        s_n = [jnp.where(new_valid[jj], jnp.sum(qrows * knew[jj:jj + 1, :], axis=-1, keepdims=True), NEG_BIG)
               for jj in range(DEC_SEQ)]
        m = jnp.max(s_c, axis=-1, keepdims=True)
        for jj in range(DEC_SEQ):
            m = jnp.maximum(m, s_n[jj])
        p_c = jnp.exp(s_c - m)
        l = jnp.sum(p_c, axis=-1, keepdims=True)
        p_cb = p_c.astype(BF16)
        if d == 1:
            vc = cache_refs[0][bi, :, GROUP_W:2 * GROUP_W].astype(BF16)
            acc = jnp.dot(p_cb, vc, preferred_element_type=F32)
        else:
            acc = jnp.zeros((rows, GROUP_W), F32)
            for j in range(DEC_SEQ):
                vc = cache_refs[j][bi, :, GROUP_W:2 * GROUP_W].astype(BF16)
                aj = jnp.dot(p_cb, vc, preferred_element_type=F32)
                acc = jnp.where(row_tok == j, aj, acc)
        for jj in range(DEC_SEQ):
            p_n = jnp.exp(s_n[jj] - m)
            l = l + p_n
            acc = acc + p_n * vnew[jj:jj + 1, :]
        o_rows = jnp.where(own, acc / l, 0.0)
        l_rows = jnp.where(own, m + jnp.log(l), 0.0)
        for j in range(DEC_SEQ):
            rs = slice(j * ATTN_HPG, (j + 1) * ATTN_HPG)
            o_out.append(jnp.sum(o_rows[rs, :], axis=0, keepdims=True))
            l_out.append(jnp.sum(l_rows[rs, :], axis=0, keepdims=True))
    o_ref[...] = jnp.concatenate(o_out, axis=0)
    l_ref[...] = jnp.concatenate(l_out, axis=0)


def _attn_sample(qk, rest, cache_v, g, layer, *, n_prompt, dec_batch):
    d = DILATIONS[g]
    bt = BT_ATTN_SAMPLE
    nk = KEYS_PER_QUERY_BLOCK
    n_res = 1 if d == 1 else DEC_SEQ
    rt = bt * DEC_SEQ
    row0 = n_prompt // rt
    rb = (rt, GROUP_W)
    out_sds = jax.ShapeDtypeStruct((dec_batch * DEC_SEQ, GROUP_W), F32)
    return pl.pallas_call(
        functools.partial(_attn_sample_body, d=d, bt=bt),
        grid=(dec_batch // bt,),
        in_specs=[
            pl.BlockSpec(rb, lambda i: (row0 + i, g)),
            pl.BlockSpec(rb, lambda i: (row0 + i, N_GROUPS + g)),
            pl.BlockSpec(rb, lambda i: (row0 + i, REST_V // GROUP_W + g)),
        ] + [pl.BlockSpec((None, bt, None, nk, 2 * GROUP_W), functools.partial(lambda i, j: (layer, i, j, 0, 0), j=j))
             for j in range(n_res)],
        out_specs=[pl.BlockSpec(rb, lambda i: (i, 0))] * 2,
        out_shape=[out_sds, out_sds],
        compiler_params=_cparams("parallel"),
    )(qk, qk, rest, *([cache_v] * n_res))


def _gated_group_norm(y, z, snw):
    yg = y * (z * _sigmoid(z))
    outs = []
    for g in range(SSM_GROUPS):
        seg = yg[:, g * SSM_GROUP_W:(g + 1) * SSM_GROUP_W]
        ms = jnp.mean(seg * seg, axis=-1, keepdims=True)
        outs.append(seg * lax.rsqrt(ms + NORM_EPS))
    return (jnp.concatenate(outs, axis=-1) * snw).astype(BF16)


def _decay_rows(cd_row):
    hi, mid, lo = _split3(cd_row)
    zeros = jnp.zeros((DECAY_ROWS - 3, cd_row.shape[1]), F32)
    return jnp.concatenate([hi, mid, lo, zeros], axis=0).astype(BF16)


def _ssd_prompt_body(xbc_ref, z_ref, dt_ref, cw_ref, cb_ref, dtb_ref, alog_ref, dsk_ref, snw_ref, e_ref, tri_ref,
                     y_ref, h_ref, carry_ref):
    q = SSD_CHUNK
    c = pl.program_id(1)

    @pl.when(c == 0)
    def _():
        h_ref[...] = jnp.zeros_like(h_ref)
        carry_ref[...] = jnp.zeros_like(carry_ref)

    x = xbc_ref[...]
    top = lax.broadcasted_iota(jnp.int32, (SUBLANES, XBC_DIM), 0) == 0

    def shift_in(t, k):
        rolled = pltpu.roll(t, 1, 0)
        head = jnp.where(top, carry_ref[k], rolled[0:SUBLANES, :])
        carry_ref[k] = rolled[0:SUBLANES, :]
        return jnp.concatenate([head, rolled[SUBLANES:, :]], axis=0)

    t = x * cw_ref[0:1, :]
    for k in range(1, CONV_WIDTH):
        t = x * cw_ref[k:k + 1, :] + shift_in(t, k - 1)
    conv = cb_ref[...] + t
    act = conv * _sigmoid(conv)
    xs = act[:, :SSM_INNER]
    bm = act[:, SSM_INNER:SSM_INNER + SSM_GROUPS * SSM_STATE]
    cm = act[:, SSM_INNER + SSM_GROUPS * SSM_STATE:]

    e_mat = e_ref[...]
    dt = _softplus(dt_ref[:, 0:LANES] + dtb_ref[...])
    a = -jnp.exp(alog_ref[...])
    da = dt * a
    acum = _dot_sel(tri_ref[...], da)
    acum_t = acum.T
    alast = acum[q - 1:q, :]
    dt_x = _expand_heads(dt, e_mat)
    de_x = _expand_heads(jnp.exp(alast - acum), e_mat)
    ea_x = _expand_heads(jnp.exp(acum), e_mat)
    cd_x = _expand_heads(jnp.broadcast_to(jnp.exp(alast), (SUBLANES, LANES)), e_mat)[0:1, :]

    xdt = xs * dt_x
    xdt_b = xdt.astype(BF16)
    xsd_b = (xdt * de_x).astype(BF16)
    li = lax.broadcasted_iota(jnp.int32, (q, q), 0)
    si = lax.broadcasted_iota(jnp.int32, (q, q), 1)
    causal = li >= si
    y_parts, st_parts = [], []
    for g in range(SSM_GROUPS):
        bg = bm[:, g * SSM_STATE:(g + 1) * SSM_STATE].astype(BF16)
        cg = cm[:, g * SSM_STATE:(g + 1) * SSM_STATE].astype(BF16)
        cb = lax.dot_general(cg, bg, (((1,), (1,)), ((), ())), preferred_element_type=F32)
        hg = h_ref[0, g * SSM_GROUP_W:(g + 1) * SSM_GROUP_W, :].astype(BF16)
        y_off = lax.dot_general(cg, hg, (((1,), (1,)), ((), ())), preferred_element_type=F32)
        yd = []
        for e in range(SSM_HPG):
            he = g * SSM_HPG + e
            seg = acum[:, he:he + 1] - acum_t[he:he + 1, :]
            lm = jnp.exp(jnp.where(causal, seg, NEG_BIG))
            mm = (cb * lm).astype(BF16)
            yd.append(jnp.dot(mm, xdt_b[:, he * SSM_HEAD_DIM:(he + 1) * SSM_HEAD_DIM], preferred_element_type=F32))
        y_parts.append(jnp.concatenate(yd, axis=-1) + y_off * ea_x[:, g * SSM_GROUP_W:(g + 1) * SSM_GROUP_W])
        st_parts.append(lax.dot_general(xsd_b[:, g * SSM_GROUP_W:(g + 1) * SSM_GROUP_W], bg,
                                        (((0,), (0,)), ((), ())), preferred_element_type=F32))
    y = jnp.concatenate(y_parts, axis=-1) + dsk_ref[...] * xs
    dmat = lax.dot_general(_decay_rows(cd_x), jnp.ones((DECAY_ROWS, SSM_STATE), BF16), (((0,), (0,)), ((), ())),
                           preferred_element_type=F32)
    h_ref[0] = h_ref[0] * dmat + jnp.concatenate(st_parts, axis=0)
    y_ref[...] = _gated_group_norm(y, z_ref[...], snw_ref[...])


def _ssd_consts():
    rows = jnp.arange(LANES)[:, None]
    head = jnp.arange(SSM_INNER)[None, :] // SSM_HEAD_DIM
    e3 = ((rows % SSM_HEADS == head) & (rows < 3 * SSM_HEADS)).astype(BF16)
    tri = (jnp.arange(SSD_CHUNK)[:, None] >= jnp.arange(SSD_CHUNK)[None, :]).astype(BF16)
    return e3, tri


def _ssd_param_specs(layer):
    fixed = (lambda *_: (layer, 0, 0))
    return [
        pl.BlockSpec((None, CONV_WIDTH, XBC_DIM), fixed),
        pl.BlockSpec((None, 1, XBC_DIM), fixed),
        pl.BlockSpec((None, 1, LANES), fixed),
        pl.BlockSpec((None, 1, LANES), fixed),
        pl.BlockSpec((None, 1, SSM_INNER), fixed),
        pl.BlockSpec((None, 1, SSM_INNER), fixed),
        pl.BlockSpec((LANES, SSM_INNER), lambda *_: (0, 0)),
    ]


def _ssd_prompt(rest, params, layer, *, batch, seq):
    q = SSD_CHUNK
    nc = seq // q
    e3, tri = _ssd_consts()
    row = lambda b, c: b * nc + c
    return pl.pallas_call(
        _ssd_prompt_body,
        grid=(batch, nc),
        in_specs=[
            pl.BlockSpec((q, XBC_DIM), lambda b, c: (row(b, c), REST_XBC // XBC_DIM)),
            pl.BlockSpec((q, SSM_INNER), lambda b, c: (row(b, c), REST_Z // SSM_INNER)),
            pl.BlockSpec((q, DT_PAD), lambda b, c: (row(b, c), REST_DT // DT_PAD)),
        ] + _ssd_param_specs(layer) + [pl.BlockSpec((q, q), lambda b, c: (0, 0))],
        out_specs=[
            pl.BlockSpec((q, SSM_INNER), lambda b, c: (row(b, c), 0)),
            pl.BlockSpec((1, SSM_INNER, SSM_STATE), lambda b, c: (b, 0, 0)),
        ],
        out_shape=[
            jax.ShapeDtypeStruct((batch * seq, SSM_INNER), BF16),
            jax.ShapeDtypeStruct((batch, SSM_INNER, SSM_STATE), F32),
        ],
        scratch_shapes=[pltpu.VMEM((CONV_WIDTH - 1, SUBLANES, XBC_DIM), F32)],
        compiler_params=_cparams("parallel", "arbitrary"),
    )(rest, rest, rest, *params, e3, tri)


def _ssd_sample_body(xbc_ref, z_ref, dt_ref, cs_ref, h0_ref, cw_ref, cb_ref, dtb_ref, alog_ref, dsk_ref, snw_ref,
                     e_ref, y_ref, h_ref, *, bt):
    r = bt * DEC_SEQ
    lrow = lax.broadcasted_iota(jnp.int32, (r, 1), 0) % DEC_SEQ
    rowb = lax.broadcasted_iota(jnp.int32, (r, 1), 0) // DEC_SEQ

    def shift(v, k):
        return v if k == 0 else pltpu.roll(v, k, 0)

    x = xbc_ref[...]
    cs = cs_ref[...]
    conv = cb_ref[...]
    for j in range(CONV_WIDTH):
        k = CONV_WIDTH - 1 - j
        if k == 0:
            term = x
        else:
            carried = cs if k == CONV_WIDTH - 1 else pltpu.roll(cs, r - (CONV_WIDTH - 1 - k), 0)
            term = jnp.where(lrow >= k, shift(x, k), carried)
        conv = conv + term * cw_ref[j:j + 1, :]
    act = conv * _sigmoid(conv)
    xs = act[:, :SSM_INNER]
    bm = act[:, SSM_INNER:SSM_INNER + SSM_GROUPS * SSM_STATE]
    cm = act[:, SSM_INNER + SSM_GROUPS * SSM_STATE:]

    e_mat = e_ref[...]
    dt = _softplus(dt_ref[:, 0:LANES] + dtb_ref[...])
    a = -jnp.exp(alog_ref[...])
    da = dt * a
    acum = da
    for k in range(1, DEC_SEQ):
        acum = acum + jnp.where(lrow >= k, shift(da, k), 0.0)
    alast = acum
    for k in range(1, DEC_SEQ):
        alast = jnp.where(lrow == DEC_SEQ - 1 - k, pltpu.roll(acum, r - k, 0), alast)

    lane_grp = lax.broadcasted_iota(jnp.int32, (r, LANES), 1) // SSM_HPG
    g_rows = []
    for k in range(DEC_SEQ):
        bk = shift(bm, k)
        cbk = jnp.zeros((r, LANES), F32)
        for g in range(SSM_GROUPS):
            sl = slice(g * SSM_STATE, (g + 1) * SSM_STATE)
            cbk = jnp.where(lane_grp == g, jnp.sum(cm[:, sl] * bk[:, sl], axis=-1, keepdims=True), cbk)
        g_rows.append(jnp.where(lrow >= k, cbk * jnp.exp(acum - shift(acum, k)), 0.0))
    stacked = jnp.concatenate([dt, jnp.exp(alast - acum), jnp.exp(acum), jnp.exp(alast)] + g_rows, axis=0)
    ex = _expand_heads(stacked, e_mat)
    dt_x, de_x, ea_x, cd_x = ex[0:r], ex[r:2 * r], ex[2 * r:3 * r], ex[3 * r:4 * r]

    xdt = xs * dt_x
    y = dsk_ref[...] * xs
    for k in range(DEC_SEQ):
        y = y + ex[(4 + k) * r:(5 + k) * r] * shift(xdt, k)

    xsd = xdt * de_x
    ones = jnp.ones((DECAY_ROWS, SSM_STATE), BF16)
    y_off = jnp.zeros((r, SSM_INNER), F32)
    for bi in range(bt):
        mine = rowb == bi
        xsd_b = jnp.where(mine, xsd, 0.0).astype(BF16)
        h0 = h0_ref[bi]
        yo, st = [], []
        for g in range(SSM_GROUPS):
            gs = slice(g * SSM_GROUP_W, (g + 1) * SSM_GROUP_W)
            bg = bm[:, g * SSM_STATE:(g + 1) * SSM_STATE].astype(BF16)
            cg = cm[:, g * SSM_STATE:(g + 1) * SSM_STATE].astype(BF16)
            yo.append(lax.dot_general(cg, h0[gs, :].astype(BF16), (((1,), (1,)), ((), ())), preferred_element_type=F32))
            st.append(lax.dot_general(xsd_b[:, gs], bg, (((0,), (0,)), ((), ())), preferred_element_type=F32))
        y_off = jnp.where(mine, jnp.concatenate(yo, axis=-1), y_off)
        dmat = lax.dot_general(_decay_rows(cd_x[bi * DEC_SEQ:bi * DEC_SEQ + 1, :]), ones, (((0,), (0,)), ((), ())),
                               preferred_element_type=F32)
        h_ref[bi] = h0 * dmat + jnp.concatenate(st, axis=0)
    y = y + y_off * ea_x
    y_ref[...] = _gated_group_norm(y, z_ref[...], snw_ref[...])


def _ssd_sample(rest, conv_state, h0, params, layer, *, n_prompt, dec_batch):
    bt = BT_SSD_SAMPLE
    r = bt * DEC_SEQ
    row0 = n_prompt // r
    e3, _ = _ssd_consts()
    return pl.pallas_call(
        functools.partial(_ssd_sample_body, bt=bt),
        grid=(dec_batch // bt,),
        in_specs=[
            pl.BlockSpec((r, XBC_DIM), lambda i: (row0 + i, REST_XBC // XBC_DIM)),
            pl.BlockSpec((r, SSM_INNER), lambda i: (row0 + i, REST_Z // SSM_INNER)),
            pl.BlockSpec((r, DT_PAD), lambda i: (row0 + i, REST_DT // DT_PAD)),
            pl.BlockSpec((None, r, XBC_DIM), lambda i: (layer, i, 0)),
            pl.BlockSpec((None, bt, SSM_INNER, SSM_STATE), lambda i: (layer, i, 0, 0)),
        ] + _ssd_param_specs(layer),
        out_specs=[
            pl.BlockSpec((r, SSM_INNER), lambda i: (i, 0)),
            pl.BlockSpec((bt, SSM_INNER, SSM_STATE), lambda i: (i, 0, 0)),
        ],
        out_shape=[
            jax.ShapeDtypeStruct((dec_batch * DEC_SEQ, SSM_INNER), BF16),
            jax.ShapeDtypeStruct((dec_batch, SSM_INNER, SSM_STATE), F32),
        ],
        compiler_params=_cparams("parallel"),
    )(rest, rest, rest, conv_state, h0, *params, e3)


def _roll_cache_body(cache_ref, new_ref, out_ref, sems, *, chunks):
    w = cache_ref.shape[2]
    copies = []
    for idx, (layer, b0, b1) in enumerate(chunks):
        copies.append(pltpu.make_async_copy(cache_ref.at[layer, b0:b1, DEC_SEQ:w],
                                            out_ref.at[layer, b0:b1, 0:w - DEC_SEQ], sems.at[0, idx]))
        copies.append(pltpu.make_async_copy(new_ref.at[layer, b0:b1],
                                            out_ref.at[layer, b0:b1, w - DEC_SEQ:w], sems.at[1, idx]))
    for c in copies:
        c.start()
    for c in copies:
        c.wait()


def _roll_cache(cache, new_rows):
    depth, b = cache.shape[:2]
    bchunk = ROLL_BATCH_CHUNK
    assert b % bchunk == 0
    chunks = [(l, b0, b0 + bchunk) for l in range(depth) for b0 in range(0, b, bchunk)]
    hbm = pl.BlockSpec(memory_space=pl.ANY)
    return pl.pallas_call(
        functools.partial(_roll_cache_body, chunks=chunks),
        in_specs=[hbm, hbm],
        out_specs=hbm,
        out_shape=jax.ShapeDtypeStruct(cache.shape, cache.dtype),
        scratch_shapes=[pltpu.SemaphoreType.DMA((2, len(chunks)))],
    )(cache, new_rows)


def _merge_body(x_ref, *refs, prompt_tiles):
    (op0, os0, op1, os1, op2, os2, lp0, ls0, lp1, ls1, lp2, ls2, yp_ref, ys_ref, g_ref,
     wao_ref, wso_ref, wo_ref, out_ref) = refs
    is_sample = pl.program_id(0) >= prompt_tiles

    def pick(p_ref, s_ref):
        return jnp.where(is_sample, s_ref[...], p_ref[...])

    l0, l1, l2 = pick(lp0, ls0), pick(lp1, ls1), pick(lp2, ls2)
    m = jnp.maximum(jnp.maximum(l0, l1), l2)
    e0, e1, e2 = jnp.exp(l0 - m), jnp.exp(l1 - m), jnp.exp(l2 - m)
    o_attn = (e0 * pick(op0, os0) + e1 * pick(op1, os1) + e2 * pick(op2, os2)) / (e0 + e1 + e2)
    att = jnp.dot(o_attn.astype(BF16), wao_ref[...], preferred_element_type=F32)
    ssm = jnp.dot(pick(yp_ref, ys_ref), wso_ref[...], preferred_element_type=F32)
    merged = _sigmoid(g_ref[:, :D_MODEL]) * att + _sigmoid(g_ref[:, D_MODEL:]) * ssm
    out_ref[...] = x_ref[...] + jnp.dot(merged.astype(BF16), wo_ref[...], preferred_element_type=F32)


def _merge(x, o_p, o_s, l_p, l_s, y_p, y_s, rest, wao, wso, wo, layer, *, tm):
    n = x.shape[0]
    n_prompt = y_p.shape[0]
    assert n_prompt % tm == 0 and y_s.shape[0] % tm == 0
    pt = n_prompt // tm
    row = lambda i: (i, 0)
    prow = lambda i: (jnp.minimum(i, pt - 1), 0)
    srow = lambda i: (jnp.maximum(i - pt, 0), 0)
    wfix = lambda i: (layer, 0, 0)

    def pair(width):
        return [pl.BlockSpec((tm, width), prow), pl.BlockSpec((tm, width), srow)]

    args = [x]
    for p, s in zip(o_p + l_p, o_s + l_s):
        args += [p, s]
    args += [y_p, y_s, rest, wao, wso, wo]
    return pl.pallas_call(
        functools.partial(_merge_body, prompt_tiles=pt),
        grid=(n // tm,),
        in_specs=[pl.BlockSpec((tm, D_MODEL), row)] + pair(GROUP_W) * 6 + pair(SSM_INNER) + [
            pl.BlockSpec((tm, 2 * D_MODEL), lambda i: (i, REST_G // (2 * D_MODEL))),
            pl.BlockSpec((None,) + wao.shape[1:], wfix),
            pl.BlockSpec((None,) + wso.shape[1:], wfix),
            pl.BlockSpec((None,) + wo.shape[1:], wfix),
        ],
        out_specs=pl.BlockSpec((tm, D_MODEL), row),
        out_shape=jax.ShapeDtypeStruct((n, D_MODEL), F32),
        compiler_params=_cparams("parallel"),
    )(*args)


def _mlp_body(x_ref, ple_ref, n2_ref, wup_ref, wdn_ref, wpg_ref, wpp_ref, out_ref):
    x = x_ref[...]
    ms = jnp.mean(x * x, axis=-1, keepdims=True)
    u = (x * lax.rsqrt(ms + NORM_EPS) * n2_ref[...]).astype(BF16)
    hdn = jnp.maximum(jnp.dot(u, wup_ref[...], preferred_element_type=F32), 0.0)
    hdn = (hdn * hdn).astype(BF16)
    x = x + jnp.dot(hdn, wdn_ref[...], preferred_element_type=F32)
    gate = _sigmoid(jnp.dot(x.astype(BF16), wpg_ref[...], preferred_element_type=F32))
    out_ref[...] = x + gate * jnp.dot(ple_ref[...].astype(BF16), wpp_ref[...], preferred_element_type=F32)


def _mlp(x, ple, n2, wup, wdn, wpg, wpp, layer, *, tm):
    n = x.shape[0]
    row = lambda i: (i, 0)
    wfix = lambda i: (layer, 0, 0)
    single = pl.Buffered(1)

    def wspec(w):
        return pl.BlockSpec((None,) + w.shape[1:], wfix, pipeline_mode=single)

    return pl.pallas_call(
        _mlp_body,
        grid=(n // tm,),
        in_specs=[
            pl.BlockSpec((tm, D_MODEL), row),
            pl.BlockSpec((None, tm, PLE_DIM), lambda i: (layer, i, 0)),
            pl.BlockSpec((None, 1, D_MODEL), wfix),
            wspec(wup), wspec(wdn), wspec(wpg), wspec(wpp),
        ],
        out_specs=pl.BlockSpec((tm, D_MODEL), row),
        out_shape=jax.ShapeDtypeStruct((n, D_MODEL), F32),
        compiler_params=_cparams("parallel"),
    )(x, ple, n2, wup, wdn, wpg, wpp)


def _final_norm_body(x_ref, w_ref, o_ref):
    x = x_ref[...]
    ms = jnp.mean(x * x, axis=-1, keepdims=True)
    o_ref[...] = x * lax.rsqrt(ms + NORM_EPS) * w_ref[...]


def _final_norm(x, w, *, tm):
    n = x.shape[0]
    return pl.pallas_call(
        _final_norm_body,
        grid=(n // tm,),
        in_specs=[pl.BlockSpec((tm, D_MODEL), lambda i: (i, 0)), pl.BlockSpec((1, D_MODEL), lambda i: (0, 0))],
        out_specs=pl.BlockSpec((tm, D_MODEL), lambda i: (i, 0)),
        out_shape=jax.ShapeDtypeStruct((n, D_MODEL), F32),
        compiler_params=_cparams("parallel"),
    )(x, w)


def _rope_tables(seq, batch, dec_batch):
    half = HEAD_DIM // 2
    inv_freq = ROPE_THETA ** (-jnp.arange(half, dtype=F32) / half)
    pos = jnp.concatenate([jnp.tile(jnp.arange(seq), batch), jnp.tile(PAST_LEN + jnp.arange(DEC_SEQ), dec_batch)])
    ang = pos.astype(F32)[:, None] * inv_freq[None, :]
    cos, sin = jnp.cos(ang), jnp.sin(ang)
    reps = LANES // HEAD_DIM
    return jnp.tile(jnp.concatenate([cos, cos], axis=1), (1, reps)), jnp.tile(jnp.concatenate([-sin, sin], axis=1), (1, reps))


def _strided_cache_view(cache, g):
    depth, b, w = cache.shape[:3]
    d = DILATIONS[g]
    assert w == WINDOWS[g] and (d == 1 or d >= DEC_SEQ)
    n_res = 1 if d == 1 else DEC_SEQ
    planes = [cache[:, :, j::d].reshape(depth, b, w // d, 2 * GROUP_W) for j in range(n_res)]
    return jnp.stack(planes, axis=2)


def kernel(x_prompt, x_sample, cache_kv_w128, cache_kv_w512, cache_kv_w2048, state_ssm, state_conv, p_prompt, p_sample, norm1_w, w_in, conv_w, conv_b, dt_bias, a_log, d_skip, ssm_norm_w, w_attn_out, w_ssm_out, w_o, norm2_w, w_up, w_down, w_ple_proj, w_ple_gate, final_norm_w):
    batch, seq, _ = x_prompt.shape
    dec_batch, dec_seq, _ = x_sample.shape
    depth = w_in.shape[0]
    assert dec_seq == DEC_SEQ and seq % (DILATIONS[-1] * KEYS_PER_QUERY_BLOCK) == 0 and seq >= WINDOWS[-1]
    n_prompt = batch * seq
    n_sample = dec_batch * dec_seq
    caches = (cache_kv_w128, cache_kv_w512, cache_kv_w2048)

    c0 = 0
    cols = {}
    for name, width in (("q", 768), ("k", 768), ("v", 768), ("z", SSM_INNER), ("xbc", XBC_DIM), ("dt", SSM_HEADS),
                        ("ga", D_MODEL), ("gb", D_MODEL)):
        cols[name] = (c0, c0 + width)
        c0 += width
    seg = lambda name: w_in[:, :, cols[name][0]:cols[name][1]]
    w_qk = w_in[:, :, :QK_W].astype(BF16)
    w_rest = jnp.concatenate(
        [seg("xbc"), seg("v"), jnp.pad(seg("dt"), ((0, 0), (0, 0), (0, DT_PAD - SSM_HEADS))), seg("z"), seg("ga"), seg("gb")],
        axis=-1).astype(BF16)
    wao, wso, wo = w_attn_out.astype(BF16), w_ssm_out.astype(BF16), w_o.astype(BF16)
    wup, wdn, wpg, wpp = w_up.astype(BF16), w_down.astype(BF16), w_ple_gate.astype(BF16), w_ple_proj.astype(BF16)
    n1 = norm1_w[:, None, :]
    n2 = norm2_w[:, None, :]
    pad_heads = lambda v: jnp.pad(v, ((0, 0), (0, LANES - SSM_HEADS)))[:, None, :]
    ssd_params = (conv_w, conv_b[:, None, :], pad_heads(dt_bias), pad_heads(a_log),
                  jnp.repeat(d_skip, SSM_HEAD_DIM, axis=-1)[:, None, :], ssm_norm_w[:, None, :])
    cos, sin = _rope_tables(seq, batch, dec_batch)
    cache_views = [_strided_cache_view(c, g) for g, c in enumerate(caches)]
    conv_carry = jnp.pad(state_conv, ((0, 0), (0, 0), (0, DEC_SEQ - (CONV_WIDTH - 1)), (0, 0))).reshape(depth, n_sample, XBC_DIM)
    h0_all = state_ssm.reshape(depth, dec_batch, SSM_INNER, SSM_STATE)

    x = jnp.concatenate([x_prompt.reshape(n_prompt, D_MODEL), x_sample.reshape(n_sample, D_MODEL)], axis=0)
    ple = jnp.concatenate([p_prompt.reshape(depth, n_prompt, PLE_DIM), p_sample.reshape(depth, n_sample, PLE_DIM)], axis=1)

    tm_proj = next(t for t in TM_PROJ_CANDIDATES if (n_prompt + n_sample) % t == 0)
    kv_p = [[] for _ in range(N_GROUPS)]
    kv_new = [[] for _ in range(N_GROUPS)]
    ssm_p, ssm_s, conv_p, conv_s = [], [], [], []
    for i in range(depth):
        qk = _inproj(x, n1, w_qk, i, cos, sin, tm=tm_proj, tn=QK_W // 2)
        rest = _inproj(x, n1, w_rest, i, tm=tm_proj, tn=1024)

        o_p, l_p, o_s, l_s = [], [], [], []
        for g in range(N_GROUPS):
            o, l = _attn_prompt(qk, rest, g, batch=batch, seq=seq)
            o_p.append(o)
            l_p.append(l)
            o, l = _attn_sample(qk, rest, cache_views[g], g, i, n_prompt=n_prompt, dec_batch=dec_batch)
            o_s.append(o)
            l_s.append(l)

        y_p, h_p = _ssd_prompt(rest, ssd_params, i, batch=batch, seq=seq)
        y_s, h_s = _ssd_sample(rest, conv_carry, h0_all, ssd_params, i, n_prompt=n_prompt, dec_batch=dec_batch)
        ssm_s.append(h_s.reshape(dec_batch, SSM_HEADS, SSM_HEAD_DIM, SSM_STATE))

        x = _merge(x, o_p, o_s, l_p, l_s, y_p, y_s, rest, wao, wso, wo, i, tm=TM_MERGE)
        x = _mlp(x, ple, n2, wup, wdn, wpg, wpp, i, tm=TM_MLP)

        for g in range(N_GROUPS):
            w = WINDOWS[g]
            ks = slice(N_GROUPS * GROUP_W + g * GROUP_W, N_GROUPS * GROUP_W + (g + 1) * GROUP_W)
            vs = slice(REST_V + g * GROUP_W, REST_V + (g + 1) * GROUP_W)
            k_tail = jnp.stack([qk[b * seq + seq - w:(b + 1) * seq, ks] for b in range(batch)])
            v_tail = jnp.stack([rest[b * seq + seq - w:(b + 1) * seq, vs] for b in range(batch)])
            kv_p[g].append(jnp.stack([k_tail, v_tail], axis=2).reshape(batch, w, 2, ATTN_HPG, HEAD_DIM))
            kvn = jnp.stack([qk[n_prompt:, ks], rest[n_prompt:, vs]], axis=1)
            kv_new[g].append(kvn.reshape(dec_batch, dec_seq, 2, ATTN_HPG, HEAD_DIM))
        tail = CONV_WIDTH - 1
        conv_p.append(jnp.stack([rest[(b + 1) * seq - tail:(b + 1) * seq, :XBC_DIM] for b in range(batch)]))
        xbc_s = rest[n_prompt:, :XBC_DIM].reshape(dec_batch, dec_seq, XBC_DIM)
        conv_s.append(jnp.concatenate([state_conv[i], xbc_s], axis=1)[:, -tail:])
        ssm_p.append(h_p.reshape(batch, SSM_HEADS, SSM_HEAD_DIM, SSM_STATE))

    y = _final_norm(x, final_norm_w[None], tm=TM_MERGE)
    y_prompt = y[:n_prompt].reshape(batch, seq, D_MODEL)
    y_sample = y[n_prompt:].reshape(dec_batch, dec_seq, D_MODEL)
    kv_s = [_roll_cache(caches[g], jnp.stack(kv_new[g])) for g in range(N_GROUPS)]
    return (y_prompt, y_sample,
            jnp.stack(kv_p[0]), jnp.stack(kv_p[1]), jnp.stack(kv_p[2]), jnp.stack(ssm_p), jnp.stack(conv_p),
            kv_s[0], kv_s[1], kv_s[2], jnp.stack(ssm_s), jnp.stack(conv_s))
```

```python
import functools

import jax
import jax.numpy as jnp
from jax import lax
from jax.experimental import pallas as pl
from jax.experimental.pallas import tpu as pltpu

F32 = jnp.float32
BF16 = jnp.bfloat16

D_MODEL = 1024
HEAD_DIM = 64
ATTN_HPG = 4
GROUP_W = ATTN_HPG * HEAD_DIM
WINDOWS = (128, 512, 2048)
DILATIONS = (1, 4, 16)
N_GROUPS = 3
KEYS_PER_QUERY_BLOCK = 128
ROPE_THETA = 10000.0
PAST_LEN = 2048
DEC_SEQ = 4

SSM_INNER = 2048
SSM_HEADS = 32
SSM_HEAD_DIM = 64
SSM_GROUPS = 4
SSM_HPG = SSM_HEADS // SSM_GROUPS
SSM_STATE = 128
SSM_GROUP_W = SSM_INNER // SSM_GROUPS
CONV_WIDTH = 4
XBC_DIM = SSM_INNER + 2 * SSM_GROUPS * SSM_STATE
SSD_CHUNK = 128
D_FF = 4 * D_MODEL
PLE_DIM = 256
NORM_EPS = 1e-6
NEG_BIG = -1e30

QK_W = 2 * N_GROUPS * GROUP_W
REST_XBC = 0
REST_V = XBC_DIM
REST_DT = REST_V + N_GROUPS * GROUP_W
REST_Z = 4096
REST_G = REST_Z + SSM_INNER
REST_W = REST_G + 2 * D_MODEL
DT_PAD = 256
LANES = 128
SUBLANES = 8
DECAY_ROWS = 16

VMEM_LIMIT = 56 * 1024 * 1024

TM_PROJ_CANDIDATES = (1536, 1024, 512)
TM_MERGE = 512
TM_MLP = 256
BT_ATTN_SAMPLE = 8
ATTN_BLOCKS_PER_STEP = 4
ATTN_RESIDUE_UNROLL = 4
BT_SSD_SAMPLE = 4


def _cparams(*sem):
    return pltpu.CompilerParams(dimension_semantics=sem, vmem_limit_bytes=VMEM_LIMIT)


def _sigmoid(x):
    return 1.0 / (1.0 + jnp.exp(-x))


def _softplus(x):
    return jnp.maximum(x, 0.0) + jnp.log(1.0 + jnp.exp(-jnp.abs(x)))


def _split3(x):
    hi = x.astype(BF16).astype(F32)
    r1 = x - hi
    mid = r1.astype(BF16).astype(F32)
    lo = (r1 - mid).astype(BF16).astype(F32)
    return hi, mid, lo


def _dot_sel(sel_bf16, x):
    hi, mid, lo = _split3(x)
    out = jnp.dot(sel_bf16, hi.astype(BF16), preferred_element_type=F32)
    out = out + jnp.dot(sel_bf16, mid.astype(BF16), preferred_element_type=F32)
    return out + jnp.dot(sel_bf16, lo.astype(BF16), preferred_element_type=F32)


def _expand_heads(v, e3_bf16):
    hi, mid, lo = _split3(v)
    lane = lax.broadcasted_iota(jnp.int32, v.shape, 1)
    packed = jnp.where(lane < SSM_HEADS, hi,
                       jnp.where(lane < 2 * SSM_HEADS, pltpu.roll(mid, SSM_HEADS, 1),
                                 jnp.where(lane < 3 * SSM_HEADS, pltpu.roll(lo, 2 * SSM_HEADS, 1), 0.0)))
    return jnp.dot(packed.astype(BF16), e3_bf16, preferred_element_type=F32)


def _inproj_body(x_ref, nw_ref, w_ref, *rest, rope, tn):
    if rope:
        cos_ref, sin_ref, o_ref, u_ref = rest
    else:
        o_ref, u_ref = rest

    @pl.when(pl.program_id(1) == 0)
    def _():
        x = x_ref[...]
        ms = jnp.mean(x * x, axis=-1, keepdims=True)
        u_ref[...] = (x * lax.rsqrt(ms + NORM_EPS) * nw_ref[...]).astype(BF16)

    acc = jnp.dot(u_ref[...], w_ref[...], preferred_element_type=F32)
    if rope:
        cos = cos_ref[...]
        sin = sin_ref[...]
        lane = lax.broadcasted_iota(jnp.int32, cos.shape, 1)
        first_half = (lane % HEAD_DIM) < (HEAD_DIM // 2)
        for c in range(tn // LANES):
            t = acc[:, c * LANES:(c + 1) * LANES]
            rot = jnp.where(first_half, pltpu.roll(t, LANES - HEAD_DIM // 2, 1), pltpu.roll(t, HEAD_DIM // 2, 1))
            o_ref[:, c * LANES:(c + 1) * LANES] = t * cos + rot * sin
    else:
        o_ref[...] = acc


def _inproj(x, nw, w, layer, cos=None, sin=None, *, tm, tn):
    n, d = x.shape
    nout = w.shape[2]
    rope = cos is not None
    in_specs = [
        pl.BlockSpec((tm, d), lambda i, j: (i, 0)),
        pl.BlockSpec((None, 1, d), lambda i, j: (layer, 0, 0)),
        pl.BlockSpec((None, d, tn), lambda i, j: (layer, 0, j)),
    ]
    args = [x, nw, w]
    if rope:
        in_specs += [pl.BlockSpec((tm, LANES), lambda i, j: (i, 0)), pl.BlockSpec((tm, LANES), lambda i, j: (i, 0))]
        args += [cos, sin]
    return pl.pallas_call(
        functools.partial(_inproj_body, rope=rope, tn=tn),
        grid=(n // tm, nout // tn),
        in_specs=in_specs,
        out_specs=pl.BlockSpec((tm, tn), lambda i, j: (i, j)),
        out_shape=jax.ShapeDtypeStruct((n, nout), F32),
        scratch_shapes=[pltpu.VMEM((tm, d), BF16)],
        compiler_params=_cparams("parallel", "arbitrary"),
    )(*args)


def _attn_prompt_body(q_ref, kp_ref, kc_ref, vp_ref, vc_ref, o_ref, l_ref, *, d, nq):
    blk = KEYS_PER_QUERY_BLOCK
    hw = q_ref.shape[1]
    n = pl.program_id(1)
    qi = lax.broadcasted_iota(jnp.int32, (blk, 2 * blk), 0)
    ki = lax.broadcasted_iota(jnp.int32, (blk, 2 * blk), 1)
    dist = blk + qi - ki
    band = (dist >= 0) & (dist <= blk)
    band_first = band & ((ki >= blk) | (n > 0))
    lane_head = lax.broadcasted_iota(jnp.int32, (blk, hw), 1) // HEAD_DIM

    def attend(q, kk, vv, mask):
        o = jnp.zeros((blk, hw), F32)
        lse = jnp.zeros((blk, hw), F32)
        for h in range(hw // HEAD_DIM):
            hm = lane_head == h
            qh = jnp.where(hm, q, 0.0).astype(BF16)
            s = lax.dot_general(qh, kk, (((1,), (1,)), ((), ())), preferred_element_type=F32)
            s = jnp.where(mask, s, NEG_BIG)
            m = jnp.max(s, axis=-1, keepdims=True)
            p = jnp.exp(s - m)
            l = jnp.sum(p, axis=-1, keepdims=True)
            oh = jnp.dot(p.astype(BF16), vv, preferred_element_type=F32)
            o = jnp.where(hm, oh / l, o)
            lse = jnp.where(hm, m + jnp.log(l), lse)
        return o, lse

    if d == 1:
        for t in range(nq):
            own = slice(t * blk, (t + 1) * blk)
            q = q_ref[own, :] * (HEAD_DIM ** -0.5)
            if t == 0:
                kk = jnp.concatenate([kp_ref[...], kc_ref[own, :]], axis=0)
                vv = jnp.concatenate([vp_ref[...], vc_ref[own, :]], axis=0)
                mask = band_first
            else:
                both = slice((t - 1) * blk, (t + 1) * blk)
                kk, vv, mask = kc_ref[both, :], vc_ref[both, :], band
            o, lse = attend(q, kk.astype(BF16), vv.astype(BF16), mask)
            o_ref[own, :] = o
            l_ref[own, :] = lse
    else:
        def residue(r, carry):
            rows = pl.ds(r, blk, stride=d)
            q = q_ref[rows, :] * (HEAD_DIM ** -0.5)
            kk = jnp.concatenate([kp_ref[rows, :], kc_ref[rows, :]], axis=0).astype(BF16)
            vv = jnp.concatenate([vp_ref[rows, :], vc_ref[rows, :]], axis=0).astype(BF16)
            o, lse = attend(q, kk, vv, band_first)
            o_ref[rows, :] = o
            l_ref[rows, :] = lse
            return carry

        lax.fori_loop(0, d, residue, 0, unroll=ATTN_RESIDUE_UNROLL)


def _attn_prompt(qk, rest, g, *, batch, seq):
    d = DILATIONS[g]
    blk = KEYS_PER_QUERY_BLOCK
    nq = ATTN_BLOCKS_PER_STEP if d == 1 else 1
    span = d * blk * nq
    nbt = seq // span
    hw = LANES
    per = GROUP_W // hw
    q_col, k_col, v_col = g * per, (N_GROUPS + g) * per, (REST_V // GROUP_W + g) * per
    prev_rows = blk if d == 1 else span

    def cur(col):
        return lambda b, i, c: (b * nbt + i, col + c)

    def prev(col):
        return lambda b, i, c: (jnp.maximum((b * nbt + i) * (span // prev_rows) - 1, 0), col + c)

    bs = (span, hw)
    ps = (prev_rows, hw)
    out_sds = jax.ShapeDtypeStruct((batch * seq, GROUP_W), F32)
    return pl.pallas_call(
        functools.partial(_attn_prompt_body, d=d, nq=nq),
        grid=(batch, nbt, per),
        in_specs=[
            pl.BlockSpec(bs, cur(q_col)),
            pl.BlockSpec(ps, prev(k_col)),
            pl.BlockSpec(bs, cur(k_col)),
            pl.BlockSpec(ps, prev(v_col)),
            pl.BlockSpec(bs, cur(v_col)),
        ],
        out_specs=[pl.BlockSpec(bs, lambda b, i, c: (b * nbt + i, c))] * 2,
        out_shape=[out_sds, out_sds],
        compiler_params=_cparams("parallel", "arbitrary", "arbitrary"),
    )(qk, qk, qk, rest, rest)


def _attn_sample_body(q_ref, kn_ref, vn_ref, *rest, d, bt):
    n_res = 1 if d == 1 else DEC_SEQ
    cache_refs = rest[:n_res]
    o_ref, l_ref = rest[n_res:]
    nk = KEYS_PER_QUERY_BLOCK
    rows = DEC_SEQ * ATTN_HPG
    row_tok = lax.broadcasted_iota(jnp.int32, (rows, 1), 0) // ATTN_HPG
    row_head = lax.broadcasted_iota(jnp.int32, (rows, GROUP_W), 0) % ATTN_HPG
    lane_head = lax.broadcasted_iota(jnp.int32, (rows, GROUP_W), 1) // HEAD_DIM
    own = lane_head == row_head
    key_row = lax.broadcasted_iota(jnp.int32, (rows, nk), 1)
    cache_valid = key_row >= (row_tok // d)
    new_valid = [(row_tok >= jj) & (((row_tok - jj) % d) == 0) for jj in range(DEC_SEQ)]
    o_out, l_out = [], []
    for bi in range(bt):
        sl = slice(bi * DEC_SEQ, (bi + 1) * DEC_SEQ)
        qb = q_ref[sl, :] * (HEAD_DIM ** -0.5)
        knew = kn_ref[sl, :]
        vnew = vn_ref[sl, :]
        qrep = jnp.concatenate([jnp.broadcast_to(qb[j:j + 1, :], (ATTN_HPG, GROUP_W)) for j in range(DEC_SEQ)], axis=0)
        qrows = jnp.where(own, qrep, 0.0)
        qrows_b = qrows.astype(BF16)
        if d == 1:
            kc = cache_refs[0][bi, :, 0:GROUP_W].astype(BF16)
            s_c = lax.dot_general(qrows_b, kc, (((1,), (1,)), ((), ())), preferred_element_type=F32)
        else:
            s_c = jnp.zeros((rows, nk), F32)
            for j in range(DEC_SEQ):
                kc = cache_refs[j][bi, :, 0:GROUP_W].astype(BF16)
                sj = lax.dot_general(qrows_b, kc, (((1,), (1,)), ((), ())), preferred_element_type=F32)
                s_c = jnp.where(row_tok == j, sj, s_c)
        s_c = jnp.where(cache_valid, s_c, NEG_BIG)
        s_n = [jnp.where(new_valid[jj], jnp.sum(qrows * knew[jj:jj + 1, :], axis=-1, keepdims=True), NEG_BIG)
               for jj in range(DEC_SEQ)]
        m = jnp.max(s_c, axis=-1, keepdims=True)
        for jj in range(DEC_SEQ):
            m = jnp.maximum(m, s_n[jj])
        p_c = jnp.exp(s_c - m)
        l = jnp.sum(p_c, axis=-1, keepdims=True)
        p_cb = p_c.astype(BF16)
        if d == 1:
            vc = cache_refs[0][bi, :, GROUP_W:2 * GROUP_W].astype(BF16)
            acc = jnp.dot(p_cb, vc, preferred_element_type=F32)
        else:
            acc = jnp.zeros((rows, GROUP_W), F32)
            for j in range(DEC_SEQ):
                vc = cache_refs[j][bi, :, GROUP_W:2 * GROUP_W].astype(BF16)
                aj = jnp.dot(p_cb, vc, preferred_element_type=F32)
                acc = jnp.where(row_tok == j, aj, acc)
        for jj in range(DEC_SEQ):
            p_n = jnp.exp(s_n[jj] - m)
            l = l + p_n
            acc = acc + p_n * vnew[jj:jj + 1, :]
        o_rows = jnp.where(own, acc / l, 0.0)
        l_rows = jnp.where(own, m + jnp.log(l), 0.0)
        for j in range(DEC_SEQ):
            rs = slice(j * ATTN_HPG, (j + 1) * ATTN_HPG)
            o_out.append(jnp.sum(o_rows[rs, :], axis=0, keepdims=True))
            l_out.append(jnp.sum(l_rows[rs, :], axis=0, keepdims=True))
    o_ref[...] = jnp.concatenate(o_out, axis=0)
    l_ref[...] = jnp.concatenate(l_out, axis=0)


def _attn_sample(qk, rest, cache_v, g, layer, *, n_prompt, dec_batch):
    d = DILATIONS[g]
    bt = BT_ATTN_SAMPLE
    nk = KEYS_PER_QUERY_BLOCK
    n_res = 1 if d == 1 else DEC_SEQ
    rt = bt * DEC_SEQ
    row0 = n_prompt // rt
    rb = (rt, GROUP_W)
    out_sds = jax.ShapeDtypeStruct((dec_batch * DEC_SEQ, GROUP_W), F32)
    return pl.pallas_call(
        functools.partial(_attn_sample_body, d=d, bt=bt),
        grid=(dec_batch // bt,),
        in_specs=[
            pl.BlockSpec(rb, lambda i: (row0 + i, g)),
            pl.BlockSpec(rb, lambda i: (row0 + i, N_GROUPS + g)),
            pl.BlockSpec(rb, lambda i: (row0 + i, REST_V // GROUP_W + g)),
        ] + [pl.BlockSpec((None, bt, nk, 2 * GROUP_W), functools.partial(lambda i, j: (layer, i, 0, j), j=j))
             for j in range(n_res)],
        out_specs=[pl.BlockSpec(rb, lambda i: (i, 0))] * 2,
        out_shape=[out_sds, out_sds],
        compiler_params=_cparams("parallel"),
    )(qk, qk, rest, *([cache_v] * n_res))


def _gated_group_norm(y, z, snw):
    yg = y * (z * _sigmoid(z))
    outs = []
    for g in range(SSM_GROUPS):
        seg = yg[:, g * SSM_GROUP_W:(g + 1) * SSM_GROUP_W]
        ms = jnp.mean(seg * seg, axis=-1, keepdims=True)
        outs.append(seg * lax.rsqrt(ms + NORM_EPS))
    return (jnp.concatenate(outs, axis=-1) * snw).astype(BF16)


def _decay_rows(cd_row):
    hi, mid, lo = _split3(cd_row)
    zeros = jnp.zeros((DECAY_ROWS - 3, cd_row.shape[1]), F32)
    return jnp.concatenate([hi, mid, lo, zeros], axis=0).astype(BF16)


def _ssd_prompt_body(xbc_ref, z_ref, dt_ref, cw_ref, cb_ref, dtb_ref, alog_ref, dsk_ref, snw_ref, e_ref, tri_ref,
                     y_ref, h_ref, carry_ref):
    q = SSD_CHUNK
    c = pl.program_id(1)

    @pl.when(c == 0)
    def _():
        h_ref[...] = jnp.zeros_like(h_ref)
        carry_ref[...] = jnp.zeros_like(carry_ref)

    x = xbc_ref[...]
    top = lax.broadcasted_iota(jnp.int32, (SUBLANES, XBC_DIM), 0) == 0

    def shift_in(t, k):
        rolled = pltpu.roll(t, 1, 0)
        head = jnp.where(top, carry_ref[k], rolled[0:SUBLANES, :])
        carry_ref[k] = rolled[0:SUBLANES, :]
        return jnp.concatenate([head, rolled[SUBLANES:, :]], axis=0)

    t = x * cw_ref[0:1, :]
    for k in range(1, CONV_WIDTH):
        t = x * cw_ref[k:k + 1, :] + shift_in(t, k - 1)
    conv = cb_ref[...] + t
    act = conv * _sigmoid(conv)
    xs = act[:, :SSM_INNER]
    bm = act[:, SSM_INNER:SSM_INNER + SSM_GROUPS * SSM_STATE]
    cm = act[:, SSM_INNER + SSM_GROUPS * SSM_STATE:]

    e_mat = e_ref[...]
    dt = _softplus(dt_ref[:, 0:LANES] + dtb_ref[...])
    a = -jnp.exp(alog_ref[...])
    da = dt * a
    acum = _dot_sel(tri_ref[...], da)
    acum_t = acum.T
    alast = acum[q - 1:q, :]
    dt_x = _expand_heads(dt, e_mat)
    de_x = _expand_heads(jnp.exp(alast - acum), e_mat)
    ea_x = _expand_heads(jnp.exp(acum), e_mat)
    cd_x = _expand_heads(jnp.broadcast_to(jnp.exp(alast), (SUBLANES, LANES)), e_mat)[0:1, :]

    xdt = xs * dt_x
    xdt_b = xdt.astype(BF16)
    xsd_b = (xdt * de_x).astype(BF16)
    li = lax.broadcasted_iota(jnp.int32, (q, q), 0)
    si = lax.broadcasted_iota(jnp.int32, (q, q), 1)
    causal = li >= si
    y_parts, st_parts = [], []
    for g in range(SSM_GROUPS):
        bg = bm[:, g * SSM_STATE:(g + 1) * SSM_STATE].astype(BF16)
        cg = cm[:, g * SSM_STATE:(g + 1) * SSM_STATE].astype(BF16)
        cb = lax.dot_general(cg, bg, (((1,), (1,)), ((), ())), preferred_element_type=F32)
        hg = h_ref[0, g * SSM_GROUP_W:(g + 1) * SSM_GROUP_W, :].astype(BF16)
        y_off = lax.dot_general(cg, hg, (((1,), (1,)), ((), ())), preferred_element_type=F32)
        yd = []
        for e in range(SSM_HPG):
            he = g * SSM_HPG + e
            seg = acum[:, he:he + 1] - acum_t[he:he + 1, :]
            lm = jnp.exp(jnp.where(causal, seg, NEG_BIG))
            mm = (cb * lm).astype(BF16)
            yd.append(jnp.dot(mm, xdt_b[:, he * SSM_HEAD_DIM:(he + 1) * SSM_HEAD_DIM], preferred_element_type=F32))
        y_parts.append(jnp.concatenate(yd, axis=-1) + y_off * ea_x[:, g * SSM_GROUP_W:(g + 1) * SSM_GROUP_W])
        st_parts.append(lax.dot_general(xsd_b[:, g * SSM_GROUP_W:(g + 1) * SSM_GROUP_W], bg,
                                        (((0,), (0,)), ((), ())), preferred_element_type=F32))
    y = jnp.concatenate(y_parts, axis=-1) + dsk_ref[...] * xs
    dmat = lax.dot_general(_decay_rows(cd_x), jnp.ones((DECAY_ROWS, SSM_STATE), BF16), (((0,), (0,)), ((), ())),
                           preferred_element_type=F32)
    h_ref[0] = h_ref[0] * dmat + jnp.concatenate(st_parts, axis=0)
    y_ref[...] = _gated_group_norm(y, z_ref[...], snw_ref[...])


def _ssd_consts():
    rows = jnp.arange(LANES)[:, None]
    head = jnp.arange(SSM_INNER)[None, :] // SSM_HEAD_DIM
    e3 = ((rows % SSM_HEADS == head) & (rows < 3 * SSM_HEADS)).astype(BF16)
    tri = (jnp.arange(SSD_CHUNK)[:, None] >= jnp.arange(SSD_CHUNK)[None, :]).astype(BF16)
    return e3, tri


def _ssd_param_specs(layer):
    fixed = (lambda *_: (layer, 0, 0))
    return [
        pl.BlockSpec((None, CONV_WIDTH, XBC_DIM), fixed),
        pl.BlockSpec((None, 1, XBC_DIM), fixed),
        pl.BlockSpec((None, 1, LANES), fixed),
        pl.BlockSpec((None, 1, LANES), fixed),
        pl.BlockSpec((None, 1, SSM_INNER), fixed),
        pl.BlockSpec((None, 1, SSM_INNER), fixed),
        pl.BlockSpec((LANES, SSM_INNER), lambda *_: (0, 0)),
    ]


def _ssd_prompt(rest, params, layer, *, batch, seq):
    q = SSD_CHUNK
    nc = seq // q
    e3, tri = _ssd_consts()
    row = lambda b, c: b * nc + c
    return pl.pallas_call(
        _ssd_prompt_body,
        grid=(batch, nc),
        in_specs=[
            pl.BlockSpec((q, XBC_DIM), lambda b, c: (row(b, c), REST_XBC // XBC_DIM)),
            pl.BlockSpec((q, SSM_INNER), lambda b, c: (row(b, c), REST_Z // SSM_INNER)),
            pl.BlockSpec((q, DT_PAD), lambda b, c: (row(b, c), REST_DT // DT_PAD)),
        ] + _ssd_param_specs(layer) + [pl.BlockSpec((q, q), lambda b, c: (0, 0))],
        out_specs=[
            pl.BlockSpec((q, SSM_INNER), lambda b, c: (row(b, c), 0)),
            pl.BlockSpec((1, SSM_INNER, SSM_STATE), lambda b, c: (b, 0, 0)),
        ],
        out_shape=[
            jax.ShapeDtypeStruct((batch * seq, SSM_INNER), BF16),
            jax.ShapeDtypeStruct((batch, SSM_INNER, SSM_STATE), F32),
        ],
        scratch_shapes=[pltpu.VMEM((CONV_WIDTH - 1, SUBLANES, XBC_DIM), F32)],
        compiler_params=_cparams("parallel", "arbitrary"),
    )(rest, rest, rest, *params, e3, tri)


def _ssd_sample_body(xbc_ref, z_ref, dt_ref, cs_ref, h0_ref, cw_ref, cb_ref, dtb_ref, alog_ref, dsk_ref, snw_ref,
                     e_ref, y_ref, h_ref, *, bt):
    r = bt * DEC_SEQ
    lrow = lax.broadcasted_iota(jnp.int32, (r, 1), 0) % DEC_SEQ
    rowb = lax.broadcasted_iota(jnp.int32, (r, 1), 0) // DEC_SEQ

    def shift(v, k):
        return v if k == 0 else pltpu.roll(v, k, 0)

    x = xbc_ref[...]
    cs = cs_ref[...]
    conv = cb_ref[...]
    for j in range(CONV_WIDTH):
        k = CONV_WIDTH - 1 - j
        if k == 0:
            term = x
        else:
            carried = cs if k == CONV_WIDTH - 1 else pltpu.roll(cs, r - (CONV_WIDTH - 1 - k), 0)
            term = jnp.where(lrow >= k, shift(x, k), carried)
        conv = conv + term * cw_ref[j:j + 1, :]
    act = conv * _sigmoid(conv)
    xs = act[:, :SSM_INNER]
    bm = act[:, SSM_INNER:SSM_INNER + SSM_GROUPS * SSM_STATE]
    cm = act[:, SSM_INNER + SSM_GROUPS * SSM_STATE:]

    e_mat = e_ref[...]
    dt = _softplus(dt_ref[:, 0:LANES] + dtb_ref[...])
    a = -jnp.exp(alog_ref[...])
    da = dt * a
    acum = da
    for k in range(1, DEC_SEQ):
        acum = acum + jnp.where(lrow >= k, shift(da, k), 0.0)
    alast = acum
    for k in range(1, DEC_SEQ):
        alast = jnp.where(lrow == DEC_SEQ - 1 - k, pltpu.roll(acum, r - k, 0), alast)

    lane_grp = lax.broadcasted_iota(jnp.int32, (r, LANES), 1) // SSM_HPG
    g_rows = []
    for k in range(DEC_SEQ):
        bk = shift(bm, k)
        cbk = jnp.zeros((r, LANES), F32)
        for g in range(SSM_GROUPS):
            sl = slice(g * SSM_STATE, (g + 1) * SSM_STATE)
            cbk = jnp.where(lane_grp == g, jnp.sum(cm[:, sl] * bk[:, sl], axis=-1, keepdims=True), cbk)
        g_rows.append(jnp.where(lrow >= k, cbk * jnp.exp(acum - shift(acum, k)), 0.0))
    stacked = jnp.concatenate([dt, jnp.exp(alast - acum), jnp.exp(acum), jnp.exp(alast)] + g_rows, axis=0)
    ex = _expand_heads(stacked, e_mat)
    dt_x, de_x, ea_x, cd_x = ex[0:r], ex[r:2 * r], ex[2 * r:3 * r], ex[3 * r:4 * r]

    xdt = xs * dt_x
    y = dsk_ref[...] * xs
    for k in range(DEC_SEQ):
        y = y + ex[(4 + k) * r:(5 + k) * r] * shift(xdt, k)

    xsd = xdt * de_x
    ones = jnp.ones((DECAY_ROWS, SSM_STATE), BF16)
    y_off = jnp.zeros((r, SSM_INNER), F32)
    for bi in range(bt):
        mine = rowb == bi
        xsd_b = jnp.where(mine, xsd, 0.0).astype(BF16)
        h0 = h0_ref[bi]
        yo, st = [], []
        for g in range(SSM_GROUPS):
            gs = slice(g * SSM_GROUP_W, (g + 1) * SSM_GROUP_W)
            bg = bm[:, g * SSM_STATE:(g + 1) * SSM_STATE].astype(BF16)
            cg = cm[:, g * SSM_STATE:(g + 1) * SSM_STATE].astype(BF16)
            yo.append(lax.dot_general(cg, h0[gs, :].astype(BF16), (((1,), (1,)), ((), ())), preferred_element_type=F32))
            st.append(lax.dot_general(xsd_b[:, gs], bg, (((0,), (0,)), ((), ())), preferred_element_type=F32))
        y_off = jnp.where(mine, jnp.concatenate(yo, axis=-1), y_off)
        dmat = lax.dot_general(_decay_rows(cd_x[bi * DEC_SEQ:bi * DEC_SEQ + 1, :]), ones, (((0,), (0,)), ((), ())),
                               preferred_element_type=F32)
        h_ref[bi] = h0 * dmat + jnp.concatenate(st, axis=0)
    y = y + y_off * ea_x
    y_ref[...] = _gated_group_norm(y, z_ref[...], snw_ref[...])


def _ssd_sample(rest, conv_state, h0, params, layer, *, n_prompt, dec_batch):
    bt = BT_SSD_SAMPLE
    r = bt * DEC_SEQ
    row0 = n_prompt // r
    e3, _ = _ssd_consts()
    return pl.pallas_call(
        functools.partial(_ssd_sample_body, bt=bt),
        grid=(dec_batch // bt,),
        in_specs=[
            pl.BlockSpec((r, XBC_DIM), lambda i: (row0 + i, REST_XBC // XBC_DIM)),
            pl.BlockSpec((r, SSM_INNER), lambda i: (row0 + i, REST_Z // SSM_INNER)),
            pl.BlockSpec((r, DT_PAD), lambda i: (row0 + i, REST_DT // DT_PAD)),
            pl.BlockSpec((None, r, XBC_DIM), lambda i: (layer, i, 0)),
            pl.BlockSpec((None, bt, SSM_INNER, SSM_STATE), lambda i: (layer, i, 0, 0)),
        ] + _ssd_param_specs(layer),
        out_specs=[
            pl.BlockSpec((r, SSM_INNER), lambda i: (i, 0)),
            pl.BlockSpec((bt, SSM_INNER, SSM_STATE), lambda i: (i, 0, 0)),
        ],
        out_shape=[
            jax.ShapeDtypeStruct((dec_batch * DEC_SEQ, SSM_INNER), BF16),
            jax.ShapeDtypeStruct((dec_batch, SSM_INNER, SSM_STATE), F32),
        ],
        compiler_params=_cparams("parallel"),
    )(rest, rest, rest, conv_state, h0, *params, e3)


def _merge_body(x_ref, *refs, prompt_tiles):
    (op0, os0, op1, os1, op2, os2, lp0, ls0, lp1, ls1, lp2, ls2, yp_ref, ys_ref, g_ref,
     wao_ref, wso_ref, wo_ref, out_ref) = refs
    is_sample = pl.program_id(0) >= prompt_tiles

    def pick(p_ref, s_ref):
        return jnp.where(is_sample, s_ref[...], p_ref[...])

    l0, l1, l2 = pick(lp0, ls0), pick(lp1, ls1), pick(lp2, ls2)
    m = jnp.maximum(jnp.maximum(l0, l1), l2)
    e0, e1, e2 = jnp.exp(l0 - m), jnp.exp(l1 - m), jnp.exp(l2 - m)
    o_attn = (e0 * pick(op0, os0) + e1 * pick(op1, os1) + e2 * pick(op2, os2)) / (e0 + e1 + e2)
    att = jnp.dot(o_attn.astype(BF16), wao_ref[...], preferred_element_type=F32)
    ssm = jnp.dot(pick(yp_ref, ys_ref), wso_ref[...], preferred_element_type=F32)
    merged = _sigmoid(g_ref[:, :D_MODEL]) * att + _sigmoid(g_ref[:, D_MODEL:]) * ssm
    out_ref[...] = x_ref[...] + jnp.dot(merged.astype(BF16), wo_ref[...], preferred_element_type=F32)


def _merge(x, o_p, o_s, l_p, l_s, y_p, y_s, rest, wao, wso, wo, layer, *, tm):
    n = x.shape[0]
    n_prompt = y_p.shape[0]
    assert n_prompt % tm == 0 and y_s.shape[0] % tm == 0
    pt = n_prompt // tm
    row = lambda i: (i, 0)
    prow = lambda i: (jnp.minimum(i, pt - 1), 0)
    srow = lambda i: (jnp.maximum(i - pt, 0), 0)
    wfix = lambda i: (layer, 0, 0)

    def pair(width):
        return [pl.BlockSpec((tm, width), prow), pl.BlockSpec((tm, width), srow)]

    args = [x]
    for p, s in zip(o_p + l_p, o_s + l_s):
        args += [p, s]
    args += [y_p, y_s, rest, wao, wso, wo]
    return pl.pallas_call(
        functools.partial(_merge_body, prompt_tiles=pt),
        grid=(n // tm,),
        in_specs=[pl.BlockSpec((tm, D_MODEL), row)] + pair(GROUP_W) * 6 + pair(SSM_INNER) + [
            pl.BlockSpec((tm, 2 * D_MODEL), lambda i: (i, REST_G // (2 * D_MODEL))),
            pl.BlockSpec((None,) + wao.shape[1:], wfix),
            pl.BlockSpec((None,) + wso.shape[1:], wfix),
            pl.BlockSpec((None,) + wo.shape[1:], wfix),
        ],
        out_specs=pl.BlockSpec((tm, D_MODEL), row),
        out_shape=jax.ShapeDtypeStruct((n, D_MODEL), F32),
        compiler_params=_cparams("parallel"),
    )(*args)


def _mlp_body(x_ref, ple_ref, n2_ref, wup_ref, wdn_ref, wpg_ref, wpp_ref, out_ref):
    x = x_ref[...]
    ms = jnp.mean(x * x, axis=-1, keepdims=True)
    u = (x * lax.rsqrt(ms + NORM_EPS) * n2_ref[...]).astype(BF16)
    hdn = jnp.maximum(jnp.dot(u, wup_ref[...], preferred_element_type=F32), 0.0)
    hdn = (hdn * hdn).astype(BF16)
    x = x + jnp.dot(hdn, wdn_ref[...], preferred_element_type=F32)
    gate = _sigmoid(jnp.dot(x.astype(BF16), wpg_ref[...], preferred_element_type=F32))
    out_ref[...] = x + gate * jnp.dot(ple_ref[...].astype(BF16), wpp_ref[...], preferred_element_type=F32)


def _mlp(x, ple, n2, wup, wdn, wpg, wpp, layer, *, tm):
    n = x.shape[0]
    row = lambda i: (i, 0)
    wfix = lambda i: (layer, 0, 0)
    single = pl.Buffered(1)

    def wspec(w):
        return pl.BlockSpec((None,) + w.shape[1:], wfix, pipeline_mode=single)

    return pl.pallas_call(
        _mlp_body,
        grid=(n // tm,),
        in_specs=[
            pl.BlockSpec((tm, D_MODEL), row),
            pl.BlockSpec((None, tm, PLE_DIM), lambda i: (layer, i, 0)),
            pl.BlockSpec((None, 1, D_MODEL), wfix),
            wspec(wup), wspec(wdn), wspec(wpg), wspec(wpp),
        ],
        out_specs=pl.BlockSpec((tm, D_MODEL), row),
        out_shape=jax.ShapeDtypeStruct((n, D_MODEL), F32),
        compiler_params=_cparams("parallel"),
    )(x, ple, n2, wup, wdn, wpg, wpp)


def _final_norm_body(x_ref, w_ref, o_ref):
    x = x_ref[...]
    ms = jnp.mean(x * x, axis=-1, keepdims=True)
    o_ref[...] = x * lax.rsqrt(ms + NORM_EPS) * w_ref[...]


def _final_norm(x, w, *, tm):
    n = x.shape[0]
    return pl.pallas_call(
        _final_norm_body,
        grid=(n // tm,),
        in_specs=[pl.BlockSpec((tm, D_MODEL), lambda i: (i, 0)), pl.BlockSpec((1, D_MODEL), lambda i: (0, 0))],
        out_specs=pl.BlockSpec((tm, D_MODEL), lambda i: (i, 0)),
        out_shape=jax.ShapeDtypeStruct((n, D_MODEL), F32),
        compiler_params=_cparams("parallel"),
    )(x, w)


def _rope_tables(seq, batch, dec_batch):
    half = HEAD_DIM // 2
    inv_freq = ROPE_THETA ** (-jnp.arange(half, dtype=F32) / half)
    pos = jnp.concatenate([jnp.tile(jnp.arange(seq), batch), jnp.tile(PAST_LEN + jnp.arange(DEC_SEQ), dec_batch)])
    ang = pos.astype(F32)[:, None] * inv_freq[None, :]
    cos, sin = jnp.cos(ang), jnp.sin(ang)
    reps = LANES // HEAD_DIM
    return jnp.tile(jnp.concatenate([cos, cos], axis=1), (1, reps)), jnp.tile(jnp.concatenate([-sin, sin], axis=1), (1, reps))


def _strided_cache_view(cache, g):
    depth, b, w = cache.shape[:3]
    d = DILATIONS[g]
    assert w == WINDOWS[g] and (d == 1 or d >= DEC_SEQ)
    n_res = 1 if d == 1 else DEC_SEQ
    v = cache.reshape(depth, b, w // d, d, 2 * GROUP_W)[:, :, :, :n_res]
    return v.reshape(depth, b, w // d, n_res * 2 * GROUP_W)


def kernel(x_prompt, x_sample, cache_kv_w128, cache_kv_w512, cache_kv_w2048, state_ssm, state_conv, p_prompt, p_sample, norm1_w, w_in, conv_w, conv_b, dt_bias, a_log, d_skip, ssm_norm_w, w_attn_out, w_ssm_out, w_o, norm2_w, w_up, w_down, w_ple_proj, w_ple_gate, final_norm_w):
    batch, seq, _ = x_prompt.shape
    dec_batch, dec_seq, _ = x_sample.shape
    depth = w_in.shape[0]
    assert dec_seq == DEC_SEQ and seq % (DILATIONS[-1] * KEYS_PER_QUERY_BLOCK) == 0 and seq >= WINDOWS[-1]
    n_prompt = batch * seq
    n_sample = dec_batch * dec_seq
    caches = (cache_kv_w128, cache_kv_w512, cache_kv_w2048)

    c0 = 0
    cols = {}
    for name, width in (("q", 768), ("k", 768), ("v", 768), ("z", SSM_INNER), ("xbc", XBC_DIM), ("dt", SSM_HEADS),
                        ("ga", D_MODEL), ("gb", D_MODEL)):
        cols[name] = (c0, c0 + width)
        c0 += width
    seg = lambda name: w_in[:, :, cols[name][0]:cols[name][1]]
    w_qk = w_in[:, :, :QK_W].astype(BF16)
    w_rest = jnp.concatenate(
        [seg("xbc"), seg("v"), jnp.pad(seg("dt"), ((0, 0), (0, 0), (0, DT_PAD - SSM_HEADS))), seg("z"), seg("ga"), seg("gb")],
        axis=-1).astype(BF16)
    wao, wso, wo = w_attn_out.astype(BF16), w_ssm_out.astype(BF16), w_o.astype(BF16)
    wup, wdn, wpg, wpp = w_up.astype(BF16), w_down.astype(BF16), w_ple_gate.astype(BF16), w_ple_proj.astype(BF16)
    n1 = norm1_w[:, None, :]
    n2 = norm2_w[:, None, :]
    pad_heads = lambda v: jnp.pad(v, ((0, 0), (0, LANES - SSM_HEADS)))[:, None, :]
    ssd_params = (conv_w, conv_b[:, None, :], pad_heads(dt_bias), pad_heads(a_log),
                  jnp.repeat(d_skip, SSM_HEAD_DIM, axis=-1)[:, None, :], ssm_norm_w[:, None, :])
    cos, sin = _rope_tables(seq, batch, dec_batch)
    cache_views = [_strided_cache_view(c, g) for g, c in enumerate(caches)]
    conv_carry = jnp.pad(state_conv, ((0, 0), (0, 0), (0, DEC_SEQ - (CONV_WIDTH - 1)), (0, 0))).reshape(depth, n_sample, XBC_DIM)
    h0_all = state_ssm.reshape(depth, dec_batch, SSM_INNER, SSM_STATE)

    x = jnp.concatenate([x_prompt.reshape(n_prompt, D_MODEL), x_sample.reshape(n_sample, D_MODEL)], axis=0)
    ple = jnp.concatenate([p_prompt.reshape(depth, n_prompt, PLE_DIM), p_sample.reshape(depth, n_sample, PLE_DIM)], axis=1)

    tm_proj = next(t for t in TM_PROJ_CANDIDATES if (n_prompt + n_sample) % t == 0)
    kv_p = [[] for _ in range(N_GROUPS)]
    kv_new = [[] for _ in range(N_GROUPS)]
    ssm_p, ssm_s, conv_p, conv_s = [], [], [], []
    for i in range(depth):
        qk = _inproj(x, n1, w_qk, i, cos, sin, tm=tm_proj, tn=QK_W // 2)
        rest = _inproj(x, n1, w_rest, i, tm=tm_proj, tn=1024)

        o_p, l_p, o_s, l_s = [], [], [], []
        for g in range(N_GROUPS):
            o, l = _attn_prompt(qk, rest, g, batch=batch, seq=seq)
            o_p.append(o)
            l_p.append(l)
            o, l = _attn_sample(qk, rest, cache_views[g], g, i, n_prompt=n_prompt, dec_batch=dec_batch)
            o_s.append(o)
            l_s.append(l)

        y_p, h_p = _ssd_prompt(rest, ssd_params, i, batch=batch, seq=seq)
        y_s, h_s = _ssd_sample(rest, conv_carry, h0_all, ssd_params, i, n_prompt=n_prompt, dec_batch=dec_batch)
        ssm_s.append(h_s.reshape(dec_batch, SSM_HEADS, SSM_HEAD_DIM, SSM_STATE))

        x = _merge(x, o_p, o_s, l_p, l_s, y_p, y_s, rest, wao, wso, wo, i, tm=TM_MERGE)
        x = _mlp(x, ple, n2, wup, wdn, wpg, wpp, i, tm=TM_MLP)

        for g in range(N_GROUPS):
            w = WINDOWS[g]
            ks = slice(N_GROUPS * GROUP_W + g * GROUP_W, N_GROUPS * GROUP_W + (g + 1) * GROUP_W)
            vs = slice(REST_V + g * GROUP_W, REST_V + (g + 1) * GROUP_W)
            k_tail = jnp.stack([qk[b * seq + seq - w:(b + 1) * seq, ks] for b in range(batch)])
            v_tail = jnp.stack([rest[b * seq + seq - w:(b + 1) * seq, vs] for b in range(batch)])
            kv_p[g].append(jnp.stack([k_tail, v_tail], axis=2).reshape(batch, w, 2, ATTN_HPG, HEAD_DIM))
            kvn = jnp.stack([qk[n_prompt:, ks], rest[n_prompt:, vs]], axis=1)
            kv_new[g].append(kvn.reshape(dec_batch, dec_seq, 2, ATTN_HPG, HEAD_DIM))
        tail = CONV_WIDTH - 1
        conv_p.append(jnp.stack([rest[(b + 1) * seq - tail:(b + 1) * seq, :XBC_DIM] for b in range(batch)]))
        xbc_s = rest[n_prompt:, :XBC_DIM].reshape(dec_batch, dec_seq, XBC_DIM)
        conv_s.append(jnp.concatenate([state_conv[i], xbc_s], axis=1)[:, -tail:])
        ssm_p.append(h_p.reshape(batch, SSM_HEADS, SSM_HEAD_DIM, SSM_STATE))

    y = _final_norm(x, final_norm_w[None], tm=TM_MERGE)
    y_prompt = y[:n_prompt].reshape(batch, seq, D_MODEL)
    y_sample = y[n_prompt:].reshape(dec_batch, dec_seq, D_MODEL)
    kv_s = []
    for g in range(N_GROUPS):
        shift = [(0, 0, 0)] * caches[g].ndim
        shift[2] = (-dec_seq, dec_seq, 0)
        rolled = lax.pad(caches[g], jnp.zeros((), caches[g].dtype), shift)
        start = (0, 0, caches[g].shape[2] - dec_seq) + (0,) * (caches[g].ndim - 3)
        kv_s.append(lax.dynamic_update_slice(rolled, jnp.stack(kv_new[g]), start))
    return (y_prompt, y_sample,
            jnp.stack(kv_p[0]), jnp.stack(kv_p[1]), jnp.stack(kv_p[2]), jnp.stack(ssm_p), jnp.stack(conv_p),
            kv_s[0], kv_s[1], kv_s[2], jnp.stack(ssm_s), jnp.stack(conv_s))
```

```python
import functools

import jax
import jax.numpy as jnp
from jax import lax
from jax.experimental import pallas as pl
from jax.experimental.pallas import tpu as pltpu

F32 = jnp.float32
BF16 = jnp.bfloat16

D_MODEL = 1024
HEAD_DIM = 64
ATTN_HPG = 4
GROUP_W = ATTN_HPG * HEAD_DIM
WINDOWS = (128, 512, 2048)
DILATIONS = (1, 4, 16)
N_GROUPS = 3
KEYS_PER_QUERY_BLOCK = 128
ROPE_THETA = 10000.0
PAST_LEN = 2048
DEC_SEQ = 4

SSM_INNER = 2048
SSM_HEADS = 32
SSM_HEAD_DIM = 64
SSM_GROUPS = 4
SSM_HPG = SSM_HEADS // SSM_GROUPS
SSM_STATE = 128
SSM_GROUP_W = SSM_INNER // SSM_GROUPS
CONV_WIDTH = 4
XBC_DIM = SSM_INNER + 2 * SSM_GROUPS * SSM_STATE
SSD_CHUNK = 128
D_FF = 4 * D_MODEL
PLE_DIM = 256
NORM_EPS = 1e-6
NEG_BIG = -1e30

QK_W = 2 * N_GROUPS * GROUP_W
REST_XBC = 0
REST_V = XBC_DIM
REST_DT = REST_V + N_GROUPS * GROUP_W
REST_Z = 4096
REST_G = REST_Z + SSM_INNER
REST_W = REST_G + 2 * D_MODEL
DT_PAD = 256
LANES = 128
SUBLANES = 8
DECAY_ROWS = 16

VMEM_LIMIT = 56 * 1024 * 1024

TM_PROJ_CANDIDATES = (1536, 1024, 512)
TM_MERGE = 512
TM_MLP = 256
BT_ATTN_SAMPLE = 8
ATTN_BLOCKS_PER_STEP = 4
ATTN_RESIDUE_UNROLL = 4
BT_SSD_SAMPLE = 4


def _cparams(*sem):
    return pltpu.CompilerParams(dimension_semantics=sem, vmem_limit_bytes=VMEM_LIMIT)


def _sigmoid(x):
    return 1.0 / (1.0 + jnp.exp(-x))


def _softplus(x):
    return jnp.maximum(x, 0.0) + jnp.log(1.0 + jnp.exp(-jnp.abs(x)))


def _split3(x):
    hi = x.astype(BF16).astype(F32)
    r1 = x - hi
    mid = r1.astype(BF16).astype(F32)
    lo = (r1 - mid).astype(BF16).astype(F32)
    return hi, mid, lo


def _dot_sel(sel_bf16, x):
    hi, mid, lo = _split3(x)
    out = jnp.dot(sel_bf16, hi.astype(BF16), preferred_element_type=F32)
    out = out + jnp.dot(sel_bf16, mid.astype(BF16), preferred_element_type=F32)
    return out + jnp.dot(sel_bf16, lo.astype(BF16), preferred_element_type=F32)


def _expand_heads(v, e3_bf16):
    hi, mid, lo = _split3(v)
    lane = lax.broadcasted_iota(jnp.int32, v.shape, 1)
    packed = jnp.where(lane < SSM_HEADS, hi,
                       jnp.where(lane < 2 * SSM_HEADS, pltpu.roll(mid, SSM_HEADS, 1),
                                 jnp.where(lane < 3 * SSM_HEADS, pltpu.roll(lo, 2 * SSM_HEADS, 1), 0.0)))
    return jnp.dot(packed.astype(BF16), e3_bf16, preferred_element_type=F32)


def _inproj_body(x_ref, nw_ref, w_ref, *rest, rope, tn):
    if rope:
        cos_ref, sin_ref, o_ref, u_ref = rest
    else:
        o_ref, u_ref = rest

    @pl.when(pl.program_id(1) == 0)
    def _():
        x = x_ref[...]
        ms = jnp.mean(x * x, axis=-1, keepdims=True)
        u_ref[...] = (x * lax.rsqrt(ms + NORM_EPS) * nw_ref[...]).astype(BF16)

    acc = jnp.dot(u_ref[...], w_ref[...], preferred_element_type=F32)
    if rope:
        cos = cos_ref[...]
        sin = sin_ref[...]
        lane = lax.broadcasted_iota(jnp.int32, cos.shape, 1)
        first_half = (lane % HEAD_DIM) < (HEAD_DIM // 2)
        for c in range(tn // LANES):
            t = acc[:, c * LANES:(c + 1) * LANES]
            rot = jnp.where(first_half, pltpu.roll(t, LANES - HEAD_DIM // 2, 1), pltpu.roll(t, HEAD_DIM // 2, 1))
            o_ref[:, c * LANES:(c + 1) * LANES] = t * cos + rot * sin
    else:
        o_ref[...] = acc


def _inproj(x, nw, w, layer, cos=None, sin=None, *, tm, tn):
    n, d = x.shape
    nout = w.shape[2]
    rope = cos is not None
    in_specs = [
        pl.BlockSpec((tm, d), lambda i, j: (i, 0)),
        pl.BlockSpec((None, 1, d), lambda i, j: (layer, 0, 0)),
        pl.BlockSpec((None, d, tn), lambda i, j: (layer, 0, j)),
    ]
    args = [x, nw, w]
    if rope:
        in_specs += [pl.BlockSpec((tm, LANES), lambda i, j: (i, 0)), pl.BlockSpec((tm, LANES), lambda i, j: (i, 0))]
        args += [cos, sin]
    return pl.pallas_call(
        functools.partial(_inproj_body, rope=rope, tn=tn),
        grid=(n // tm, nout // tn),
        in_specs=in_specs,
        out_specs=pl.BlockSpec((tm, tn), lambda i, j: (i, j)),
        out_shape=jax.ShapeDtypeStruct((n, nout), F32),
        scratch_shapes=[pltpu.VMEM((tm, d), BF16)],
        compiler_params=_cparams("parallel", "arbitrary"),
    )(*args)


def _attn_prompt_body(q_ref, kp_ref, kc_ref, vp_ref, vc_ref, o_ref, l_ref, *, d, nq):
    blk = KEYS_PER_QUERY_BLOCK
    hw = q_ref.shape[1]
    n = pl.program_id(1)
    qi = lax.broadcasted_iota(jnp.int32, (blk, 2 * blk), 0)
    ki = lax.broadcasted_iota(jnp.int32, (blk, 2 * blk), 1)
    dist = blk + qi - ki
    band = (dist >= 0) & (dist <= blk)
    band_first = band & ((ki >= blk) | (n > 0))
    lane_head = lax.broadcasted_iota(jnp.int32, (blk, hw), 1) // HEAD_DIM

    def attend(q, kk, vv, mask):
        o = jnp.zeros((blk, hw), F32)
        lse = jnp.zeros((blk, hw), F32)
        for h in range(hw // HEAD_DIM):
            hm = lane_head == h
            qh = jnp.where(hm, q, 0.0).astype(BF16)
            s = lax.dot_general(qh, kk, (((1,), (1,)), ((), ())), preferred_element_type=F32)
            s = jnp.where(mask, s, NEG_BIG)
            m = jnp.max(s, axis=-1, keepdims=True)
            p = jnp.exp(s - m)
            l = jnp.sum(p, axis=-1, keepdims=True)
            oh = jnp.dot(p.astype(BF16), vv, preferred_element_type=F32)
            o = jnp.where(hm, oh / l, o)
            lse = jnp.where(hm, m + jnp.log(l), lse)
        return o, lse

    if d == 1:
        for t in range(nq):
            own = slice(t * blk, (t + 1) * blk)
            q = q_ref[own, :] * (HEAD_DIM ** -0.5)
            if t == 0:
                kk = jnp.concatenate([kp_ref[...], kc_ref[own, :]], axis=0)
                vv = jnp.concatenate([vp_ref[...], vc_ref[own, :]], axis=0)
                mask = band_first
            else:
                both = slice((t - 1) * blk, (t + 1) * blk)
                kk, vv, mask = kc_ref[both, :], vc_ref[both, :], band
            o, lse = attend(q, kk.astype(BF16), vv.astype(BF16), mask)
            o_ref[own, :] = o
            l_ref[own, :] = lse
    else:
        def residue(r, carry):
            rows = pl.ds(r, blk, stride=d)
            q = q_ref[rows, :] * (HEAD_DIM ** -0.5)
            kk = jnp.concatenate([kp_ref[rows, :], kc_ref[rows, :]], axis=0).astype(BF16)
            vv = jnp.concatenate([vp_ref[rows, :], vc_ref[rows, :]], axis=0).astype(BF16)
            o, lse = attend(q, kk, vv, band_first)
            o_ref[rows, :] = o
            l_ref[rows, :] = lse
            return carry

        lax.fori_loop(0, d, residue, 0, unroll=ATTN_RESIDUE_UNROLL)


def _attn_prompt(qk, rest, g, *, batch, seq):
    d = DILATIONS[g]
    blk = KEYS_PER_QUERY_BLOCK
    nq = ATTN_BLOCKS_PER_STEP if d == 1 else 1
    span = d * blk * nq
    nbt = seq // span
    hw = LANES
    per = GROUP_W // hw
    q_col, k_col, v_col = g * per, (N_GROUPS + g) * per, (REST_V // GROUP_W + g) * per
    prev_rows = blk if d == 1 else span

    def cur(col):
        return lambda b, i, c: (b * nbt + i, col + c)

    def prev(col):
        return lambda b, i, c: (jnp.maximum((b * nbt + i) * (span // prev_rows) - 1, 0), col + c)

    bs = (span, hw)
    ps = (prev_rows, hw)
    out_sds = jax.ShapeDtypeStruct((batch * seq, GROUP_W), F32)
    return pl.pallas_call(
        functools.partial(_attn_prompt_body, d=d, nq=nq),
        grid=(batch, nbt, per),
        in_specs=[
            pl.BlockSpec(bs, cur(q_col)),
            pl.BlockSpec(ps, prev(k_col)),
            pl.BlockSpec(bs, cur(k_col)),
            pl.BlockSpec(ps, prev(v_col)),
            pl.BlockSpec(bs, cur(v_col)),
        ],
        out_specs=[pl.BlockSpec(bs, lambda b, i, c: (b * nbt + i, c))] * 2,
        out_shape=[out_sds, out_sds],
        compiler_params=_cparams("parallel", "arbitrary", "arbitrary"),
    )(qk, qk, qk, rest, rest)


def _attn_sample_body(q_ref, kn_ref, vn_ref, *rest, d, bt):
    n_res = 1 if d == 1 else DEC_SEQ
    cache_refs = rest[:n_res]
    o_ref, l_ref = rest[n_res:]
    nk = KEYS_PER_QUERY_BLOCK
    rows = DEC_SEQ * ATTN_HPG
    row_tok = lax.broadcasted_iota(jnp.int32, (rows, 1), 0) // ATTN_HPG
    row_head = lax.broadcasted_iota(jnp.int32, (rows, GROUP_W), 0) % ATTN_HPG
    lane_head = lax.broadcasted_iota(jnp.int32, (rows, GROUP_W), 1) // HEAD_DIM
    own = lane_head == row_head
    key_row = lax.broadcasted_iota(jnp.int32, (rows, nk), 1)
    cache_valid = key_row >= (row_tok // d)
    new_valid = [(row_tok >= jj) & (((row_tok - jj) % d) == 0) for jj in range(DEC_SEQ)]
    o_out, l_out = [], []
    for bi in range(bt):
        sl = slice(bi * DEC_SEQ, (bi + 1) * DEC_SEQ)
        qb = q_ref[sl, :] * (HEAD_DIM ** -0.5)
        knew = kn_ref[sl, :]
        vnew = vn_ref[sl, :]
        qrep = jnp.concatenate([jnp.broadcast_to(qb[j:j + 1, :], (ATTN_HPG, GROUP_W)) for j in range(DEC_SEQ)], axis=0)
        qrows = jnp.where(own, qrep, 0.0)
        qrows_b = qrows.astype(BF16)
        if d == 1:
            kc = cache_refs[0][bi, :, 0:GROUP_W].astype(BF16)
            s_c = lax.dot_general(qrows_b, kc, (((1,), (1,)), ((), ())), preferred_element_type=F32)
        else:
            s_c = jnp.zeros((rows, nk), F32)
            for j in range(DEC_SEQ):
                kc = cache_refs[j][bi, :, 0:GROUP_W].astype(BF16)
                sj = lax.dot_general(qrows_b, kc, (((1,), (1,)), ((), ())), preferred_element_type=F32)
                s_c = jnp.where(row_tok == j, sj, s_c)
        s_c = jnp.where(cache_valid, s_c, NEG_BIG)
        s_n = [jnp.where(new_valid[jj], jnp.sum(qrows * knew[jj:jj + 1, :], axis=-1, keepdims=True), NEG_BIG)
               for jj in range(DEC_SEQ)]
        m = jnp.max(s_c, axis=-1, keepdims=True)
        for jj in range(DEC_SEQ):
            m = jnp.maximum(m, s_n[jj])
        p_c = jnp.exp(s_c - m)
        l = jnp.sum(p_c, axis=-1, keepdims=True)
        p_cb = p_c.astype(BF16)
        if d == 1:
            vc = cache_refs[0][bi, :, GROUP_W:2 * GROUP_W].astype(BF16)
            acc = jnp.dot(p_cb, vc, preferred_element_type=F32)
        else:
            acc = jnp.zeros((rows, GROUP_W), F32)
            for j in range(DEC_SEQ):
                vc = cache_refs[j][bi, :, GROUP_W:2 * GROUP_W].astype(BF16)
                aj = jnp.dot(p_cb, vc, preferred_element_type=F32)
                acc = jnp.where(row_tok == j, aj, acc)
        for jj in range(DEC_SEQ):
            p_n = jnp.exp(s_n[jj] - m)
            l = l + p_n
            acc = acc + p_n * vnew[jj:jj + 1, :]
        o_rows = jnp.where(own, acc / l, 0.0)
        l_rows = jnp.where(own, m + jnp.log(l), 0.0)
        for j in range(DEC_SEQ):
            rs = slice(j * ATTN_HPG, (j + 1) * ATTN_HPG)
            o_out.append(jnp.sum(o_rows[rs, :], axis=0, keepdims=True))
            l_out.append(jnp.sum(l_rows[rs, :], axis=0, keepdims=True))
    o_ref[...] = jnp.concatenate(o_out, axis=0)
    l_ref[...] = jnp.concatenate(l_out, axis=0)


def _attn_sample(qk, rest, cache_v, g, layer, *, n_prompt, dec_batch):
    d = DILATIONS[g]
    bt = BT_ATTN_SAMPLE
    nk = KEYS_PER_QUERY_BLOCK
    n_res = 1 if d == 1 else DEC_SEQ
    rt = bt * DEC_SEQ
    row0 = n_prompt // rt
    rb = (rt, GROUP_W)
    out_sds = jax.ShapeDtypeStruct((dec_batch * DEC_SEQ, GROUP_W), F32)
    return pl.pallas_call(
        functools.partial(_attn_sample_body, d=d, bt=bt),
        grid=(dec_batch // bt,),
        in_specs=[
            pl.BlockSpec(rb, lambda i: (row0 + i, g)),
            pl.BlockSpec(rb, lambda i: (row0 + i, N_GROUPS + g)),
            pl.BlockSpec(rb, lambda i: (row0 + i, REST_V // GROUP_W + g)),
        ] + [pl.BlockSpec((None, bt, nk, 2 * GROUP_W), functools.partial(lambda i, j: (layer, i, 0, j), j=j))
             for j in range(n_res)],
        out_specs=[pl.BlockSpec(rb, lambda i: (i, 0))] * 2,
        out_shape=[out_sds, out_sds],
        compiler_params=_cparams("parallel"),
    )(qk, qk, rest, *([cache_v] * n_res))


def _gated_group_norm(y, z, snw):
    yg = y * (z * _sigmoid(z))
    outs = []
    for g in range(SSM_GROUPS):
        seg = yg[:, g * SSM_GROUP_W:(g + 1) * SSM_GROUP_W]
        ms = jnp.mean(seg * seg, axis=-1, keepdims=True)
        outs.append(seg * lax.rsqrt(ms + NORM_EPS))
    return (jnp.concatenate(outs, axis=-1) * snw).astype(BF16)


def _decay_rows(cd_row):
    hi, mid, lo = _split3(cd_row)
    zeros = jnp.zeros((DECAY_ROWS - 3, cd_row.shape[1]), F32)
    return jnp.concatenate([hi, mid, lo, zeros], axis=0).astype(BF16)


def _ssd_prompt_body(xbc_ref, z_ref, dt_ref, cw_ref, cb_ref, dtb_ref, alog_ref, dsk_ref, snw_ref, e_ref, tri_ref,
                     y_ref, h_ref, carry_ref):
    q = SSD_CHUNK
    c = pl.program_id(1)

    @pl.when(c == 0)
    def _():
        h_ref[...] = jnp.zeros_like(h_ref)
        carry_ref[...] = jnp.zeros_like(carry_ref)

    x = xbc_ref[...]
    top = lax.broadcasted_iota(jnp.int32, (SUBLANES, XBC_DIM), 0) == 0

    def shift_in(t, k):
        rolled = pltpu.roll(t, 1, 0)
        head = jnp.where(top, carry_ref[k], rolled[0:SUBLANES, :])
        carry_ref[k] = rolled[0:SUBLANES, :]
        return jnp.concatenate([head, rolled[SUBLANES:, :]], axis=0)

    t = x * cw_ref[0:1, :]
    for k in range(1, CONV_WIDTH):
        t = x * cw_ref[k:k + 1, :] + shift_in(t, k - 1)
    conv = cb_ref[...] + t
    act = conv * _sigmoid(conv)
    xs = act[:, :SSM_INNER]
    bm = act[:, SSM_INNER:SSM_INNER + SSM_GROUPS * SSM_STATE]
    cm = act[:, SSM_INNER + SSM_GROUPS * SSM_STATE:]

    e_mat = e_ref[...]
    dt = _softplus(dt_ref[:, 0:LANES] + dtb_ref[...])
    a = -jnp.exp(alog_ref[...])
    da = dt * a
    acum = _dot_sel(tri_ref[...], da)
    acum_t = acum.T
    alast = acum[q - 1:q, :]
    dt_x = _expand_heads(dt, e_mat)
    de_x = _expand_heads(jnp.exp(alast - acum), e_mat)
    ea_x = _expand_heads(jnp.exp(acum), e_mat)
    cd_x = _expand_heads(jnp.broadcast_to(jnp.exp(alast), (SUBLANES, LANES)), e_mat)[0:1, :]

    xdt = xs * dt_x
    xdt_b = xdt.astype(BF16)
    xsd_b = (xdt * de_x).astype(BF16)
    li = lax.broadcasted_iota(jnp.int32, (q, q), 0)
    si = lax.broadcasted_iota(jnp.int32, (q, q), 1)
    causal = li >= si
    y_parts, st_parts = [], []
    for g in range(SSM_GROUPS):
        bg = bm[:, g * SSM_STATE:(g + 1) * SSM_STATE].astype(BF16)
        cg = cm[:, g * SSM_STATE:(g + 1) * SSM_STATE].astype(BF16)
        cb = lax.dot_general(cg, bg, (((1,), (1,)), ((), ())), preferred_element_type=F32)
        hg = h_ref[0, g * SSM_GROUP_W:(g + 1) * SSM_GROUP_W, :].astype(BF16)
        y_off = lax.dot_general(cg, hg, (((1,), (1,)), ((), ())), preferred_element_type=F32)
        yd = []
        for e in range(SSM_HPG):
            he = g * SSM_HPG + e
            seg = acum[:, he:he + 1] - acum_t[he:he + 1, :]
            lm = jnp.exp(jnp.where(causal, seg, NEG_BIG))
            mm = (cb * lm).astype(BF16)
            yd.append(jnp.dot(mm, xdt_b[:, he * SSM_HEAD_DIM:(he + 1) * SSM_HEAD_DIM], preferred_element_type=F32))
        y_parts.append(jnp.concatenate(yd, axis=-1) + y_off * ea_x[:, g * SSM_GROUP_W:(g + 1) * SSM_GROUP_W])
        st_parts.append(lax.dot_general(xsd_b[:, g * SSM_GROUP_W:(g + 1) * SSM_GROUP_W], bg,
                                        (((0,), (0,)), ((), ())), preferred_element_type=F32))
    y = jnp.concatenate(y_parts, axis=-1) + dsk_ref[...] * xs
    dmat = lax.dot_general(_decay_rows(cd_x), jnp.ones((DECAY_ROWS, SSM_STATE), BF16), (((0,), (0,)), ((), ())),
                           preferred_element_type=F32)
    h_ref[0] = h_ref[0] * dmat + jnp.concatenate(st_parts, axis=0)
    y_ref[...] = _gated_group_norm(y, z_ref[...], snw_ref[...])


def _ssd_consts():
    rows = jnp.arange(LANES)[:, None]
    head = jnp.arange(SSM_INNER)[None, :] // SSM_HEAD_DIM
    e3 = ((rows % SSM_HEADS == head) & (rows < 3 * SSM_HEADS)).astype(BF16)
    tri = (jnp.arange(SSD_CHUNK)[:, None] >= jnp.arange(SSD_CHUNK)[None, :]).astype(BF16)
    return e3, tri


def _ssd_param_specs(layer):
    fixed = (lambda *_: (layer, 0, 0))
    return [
        pl.BlockSpec((None, CONV_WIDTH, XBC_DIM), fixed),
        pl.BlockSpec((None, 1, XBC_DIM), fixed),
        pl.BlockSpec((None, 1, LANES), fixed),
        pl.BlockSpec((None, 1, LANES), fixed),
        pl.BlockSpec((None, 1, SSM_INNER), fixed),
        pl.BlockSpec((None, 1, SSM_INNER), fixed),
        pl.BlockSpec((LANES, SSM_INNER), lambda *_: (0, 0)),
    ]


def _ssd_prompt(rest, params, layer, *, batch, seq):
    q = SSD_CHUNK
    nc = seq // q
    e3, tri = _ssd_consts()
    row = lambda b, c: b * nc + c
    return pl.pallas_call(
        _ssd_prompt_body,
        grid=(batch, nc),
        in_specs=[
            pl.BlockSpec((q, XBC_DIM), lambda b, c: (row(b, c), REST_XBC // XBC_DIM)),
            pl.BlockSpec((q, SSM_INNER), lambda b, c: (row(b, c), REST_Z // SSM_INNER)),
            pl.BlockSpec((q, DT_PAD), lambda b, c: (row(b, c), REST_DT // DT_PAD)),
        ] + _ssd_param_specs(layer) + [pl.BlockSpec((q, q), lambda b, c: (0, 0))],
        out_specs=[
            pl.BlockSpec((q, SSM_INNER), lambda b, c: (row(b, c), 0)),
            pl.BlockSpec((1, SSM_INNER, SSM_STATE), lambda b, c: (b, 0, 0)),
        ],
        out_shape=[
            jax.ShapeDtypeStruct((batch * seq, SSM_INNER), BF16),
            jax.ShapeDtypeStruct((batch, SSM_INNER, SSM_STATE), F32),
        ],
        scratch_shapes=[pltpu.VMEM((CONV_WIDTH - 1, SUBLANES, XBC_DIM), F32)],
        compiler_params=_cparams("parallel", "arbitrary"),
    )(rest, rest, rest, *params, e3, tri)


def _ssd_sample_body(xbc_ref, z_ref, dt_ref, cs_ref, h0_ref, cw_ref, cb_ref, dtb_ref, alog_ref, dsk_ref, snw_ref,
                     e_ref, y_ref, h_ref, *, bt):
    r = bt * DEC_SEQ
    lrow = lax.broadcasted_iota(jnp.int32, (r, 1), 0) % DEC_SEQ
    rowb = lax.broadcasted_iota(jnp.int32, (r, 1), 0) // DEC_SEQ

    def shift(v, k):
        return v if k == 0 else pltpu.roll(v, k, 0)

    x = xbc_ref[...]
    cs = cs_ref[...]
    conv = cb_ref[...]
    for j in range(CONV_WIDTH):
        k = CONV_WIDTH - 1 - j
        if k == 0:
            term = x
        else:
            carried = cs if k == CONV_WIDTH - 1 else pltpu.roll(cs, r - (CONV_WIDTH - 1 - k), 0)
            term = jnp.where(lrow >= k, shift(x, k), carried)
        conv = conv + term * cw_ref[j:j + 1, :]
    act = conv * _sigmoid(conv)
    xs = act[:, :SSM_INNER]
    bm = act[:, SSM_INNER:SSM_INNER + SSM_GROUPS * SSM_STATE]
    cm = act[:, SSM_INNER + SSM_GROUPS * SSM_STATE:]

    e_mat = e_ref[...]
    dt = _softplus(dt_ref[:, 0:LANES] + dtb_ref[...])
    a = -jnp.exp(alog_ref[...])
    da = dt * a
    acum = da
    for k in range(1, DEC_SEQ):
        acum = acum + jnp.where(lrow >= k, shift(da, k), 0.0)
    alast = acum
    for k in range(1, DEC_SEQ):
        alast = jnp.where(lrow == DEC_SEQ - 1 - k, pltpu.roll(acum, r - k, 0), alast)

    lane_grp = lax.broadcasted_iota(jnp.int32, (r, LANES), 1) // SSM_HPG
    g_rows = []
    for k in range(DEC_SEQ):
        bk = shift(bm, k)
        cbk = jnp.zeros((r, LANES), F32)
        for g in range(SSM_GROUPS):
            sl = slice(g * SSM_STATE, (g + 1) * SSM_STATE)
            cbk = jnp.where(lane_grp == g, jnp.sum(cm[:, sl] * bk[:, sl], axis=-1, keepdims=True), cbk)
        g_rows.append(jnp.where(lrow >= k, cbk * jnp.exp(acum - shift(acum, k)), 0.0))
    stacked = jnp.concatenate([dt, jnp.exp(alast - acum), jnp.exp(acum), jnp.exp(alast)] + g_rows, axis=0)
    ex = _expand_heads(stacked, e_mat)
    dt_x, de_x, ea_x, cd_x = ex[0:r], ex[r:2 * r], ex[2 * r:3 * r], ex[3 * r:4 * r]

    xdt = xs * dt_x
    y = dsk_ref[...] * xs
    for k in range(DEC_SEQ):
        y = y + ex[(4 + k) * r:(5 + k) * r] * shift(xdt, k)

    xsd = xdt * de_x
    ones = jnp.ones((DECAY_ROWS, SSM_STATE), BF16)
    y_off = jnp.zeros((r, SSM_INNER), F32)
    for bi in range(bt):
        mine = rowb == bi
        xsd_b = jnp.where(mine, xsd, 0.0).astype(BF16)
        h0 = h0_ref[bi]
        yo, st = [], []
        for g in range(SSM_GROUPS):
            gs = slice(g * SSM_GROUP_W, (g + 1) * SSM_GROUP_W)
            bg = bm[:, g * SSM_STATE:(g + 1) * SSM_STATE].astype(BF16)
            cg = cm[:, g * SSM_STATE:(g + 1) * SSM_STATE].astype(BF16)
            yo.append(lax.dot_general(cg, h0[gs, :].astype(BF16), (((1,), (1,)), ((), ())), preferred_element_type=F32))
            st.append(lax.dot_general(xsd_b[:, gs], bg, (((0,), (0,)), ((), ())), preferred_element_type=F32))
        y_off = jnp.where(mine, jnp.concatenate(yo, axis=-1), y_off)
        dmat = lax.dot_general(_decay_rows(cd_x[bi * DEC_SEQ:bi * DEC_SEQ + 1, :]), ones, (((0,), (0,)), ((), ())),
                               preferred_element_type=F32)
        h_ref[bi] = h0 * dmat + jnp.concatenate(st, axis=0)
    y = y + y_off * ea_x
    y_ref[...] = _gated_group_norm(y, z_ref[...], snw_ref[...])


def _ssd_sample(rest, conv_state, h0, params, layer, *, n_prompt, dec_batch):
    bt = BT_SSD_SAMPLE
    r = bt * DEC_SEQ
    row0 = n_prompt // r
    e3, _ = _ssd_consts()
    return pl.pallas_call(
        functools.partial(_ssd_sample_body, bt=bt),
        grid=(dec_batch // bt,),
        in_specs=[
            pl.BlockSpec((r, XBC_DIM), lambda i: (row0 + i, REST_XBC // XBC_DIM)),
            pl.BlockSpec((r, SSM_INNER), lambda i: (row0 + i, REST_Z // SSM_INNER)),
            pl.BlockSpec((r, DT_PAD), lambda i: (row0 + i, REST_DT // DT_PAD)),
            pl.BlockSpec((None, r, XBC_DIM), lambda i: (layer, i, 0)),
            pl.BlockSpec((None, bt, SSM_INNER, SSM_STATE), lambda i: (layer, i, 0, 0)),
        ] + _ssd_param_specs(layer),
        out_specs=[
            pl.BlockSpec((r, SSM_INNER), lambda i: (i, 0)),
            pl.BlockSpec((bt, SSM_INNER, SSM_STATE), lambda i: (i, 0, 0)),
        ],
        out_shape=[
            jax.ShapeDtypeStruct((dec_batch * DEC_SEQ, SSM_INNER), BF16),
            jax.ShapeDtypeStruct((dec_batch, SSM_INNER, SSM_STATE), F32),
        ],
        compiler_params=_cparams("parallel"),
    )(rest, rest, rest, conv_state, h0, *params, e3)


def _merge_body(x_ref, *refs, prompt_tiles):
    (op0, os0, op1, os1, op2, os2, lp0, ls0, lp1, ls1, lp2, ls2, yp_ref, ys_ref, g_ref,
     wao_ref, wso_ref, wo_ref, out_ref) = refs
    is_sample = pl.program_id(0) >= prompt_tiles

    def pick(p_ref, s_ref):
        return jnp.where(is_sample, s_ref[...], p_ref[...])

    l0, l1, l2 = pick(lp0, ls0), pick(lp1, ls1), pick(lp2, ls2)
    m = jnp.maximum(jnp.maximum(l0, l1), l2)
    e0, e1, e2 = jnp.exp(l0 - m), jnp.exp(l1 - m), jnp.exp(l2 - m)
    o_attn = (e0 * pick(op0, os0) + e1 * pick(op1, os1) + e2 * pick(op2, os2)) / (e0 + e1 + e2)
    att = jnp.dot(o_attn.astype(BF16), wao_ref[...], preferred_element_type=F32)
    ssm = jnp.dot(pick(yp_ref, ys_ref), wso_ref[...], preferred_element_type=F32)
    merged = _sigmoid(g_ref[:, :D_MODEL]) * att + _sigmoid(g_ref[:, D_MODEL:]) * ssm
    out_ref[...] = x_ref[...] + jnp.dot(merged.astype(BF16), wo_ref[...], preferred_element_type=F32)


def _merge(x, o_p, o_s, l_p, l_s, y_p, y_s, rest, wao, wso, wo, layer, *, tm):
    n = x.shape[0]
    n_prompt = y_p.shape[0]
    assert n_prompt % tm == 0 and y_s.shape[0] % tm == 0
    pt = n_prompt // tm
    row = lambda i: (i, 0)
    prow = lambda i: (jnp.minimum(i, pt - 1), 0)
    srow = lambda i: (jnp.maximum(i - pt, 0), 0)
    wfix = lambda i: (layer, 0, 0)

    def pair(width):
        return [pl.BlockSpec((tm, width), prow), pl.BlockSpec((tm, width), srow)]

    args = [x]
    for p, s in zip(o_p + l_p, o_s + l_s):
        args += [p, s]
    args += [y_p, y_s, rest, wao, wso, wo]
    return pl.pallas_call(
        functools.partial(_merge_body, prompt_tiles=pt),
        grid=(n // tm,),
        in_specs=[pl.BlockSpec((tm, D_MODEL), row)] + pair(GROUP_W) * 6 + pair(SSM_INNER) + [
            pl.BlockSpec((tm, 2 * D_MODEL), lambda i: (i, REST_G // (2 * D_MODEL))),
            pl.BlockSpec((None,) + wao.shape[1:], wfix),
            pl.BlockSpec((None,) + wso.shape[1:], wfix),
            pl.BlockSpec((None,) + wo.shape[1:], wfix),
        ],
        out_specs=pl.BlockSpec((tm, D_MODEL), row),
        out_shape=jax.ShapeDtypeStruct((n, D_MODEL), F32),
        compiler_params=_cparams("parallel"),
    )(*args)


def _mlp_body(x_ref, ple_ref, n2_ref, wup_ref, wdn_ref, wpg_ref, wpp_ref, out_ref):
    x = x_ref[...]
    ms = jnp.mean(x * x, axis=-1, keepdims=True)
    u = (x * lax.rsqrt(ms + NORM_EPS) * n2_ref[...]).astype(BF16)
    hdn = jnp.maximum(jnp.dot(u, wup_ref[...], preferred_element_type=F32), 0.0)
    hdn = (hdn * hdn).astype(BF16)
    x = x + jnp.dot(hdn, wdn_ref[...], preferred_element_type=F32)
    gate = _sigmoid(jnp.dot(x.astype(BF16), wpg_ref[...], preferred_element_type=F32))
    out_ref[...] = x + gate * jnp.dot(ple_ref[...].astype(BF16), wpp_ref[...], preferred_element_type=F32)


def _mlp(x, ple, n2, wup, wdn, wpg, wpp, layer, *, tm):
    n = x.shape[0]
    row = lambda i: (i, 0)
    wfix = lambda i: (layer, 0, 0)
    single = pl.Buffered(1)

    def wspec(w):
        return pl.BlockSpec((None,) + w.shape[1:], wfix, pipeline_mode=single)

    return pl.pallas_call(
        _mlp_body,
        grid=(n // tm,),
        in_specs=[
            pl.BlockSpec((tm, D_MODEL), row),
            pl.BlockSpec((None, tm, PLE_DIM), lambda i: (layer, i, 0)),
            pl.BlockSpec((None, 1, D_MODEL), wfix),
            wspec(wup), wspec(wdn), wspec(wpg), wspec(wpp),
        ],
        out_specs=pl.BlockSpec((tm, D_MODEL), row),
        out_shape=jax.ShapeDtypeStruct((n, D_MODEL), F32),
        compiler_params=_cparams("parallel"),
    )(x, ple, n2, wup, wdn, wpg, wpp)


def _final_norm_body(x_ref, w_ref, o_ref):
    x = x_ref[...]
    ms = jnp.mean(x * x, axis=-1, keepdims=True)
    o_ref[...] = x * lax.rsqrt(ms + NORM_EPS) * w_ref[...]


def _final_norm(x, w, *, tm):
    n = x.shape[0]
    return pl.pallas_call(
        _final_norm_body,
        grid=(n // tm,),
        in_specs=[pl.BlockSpec((tm, D_MODEL), lambda i: (i, 0)), pl.BlockSpec((1, D_MODEL), lambda i: (0, 0))],
        out_specs=pl.BlockSpec((tm, D_MODEL), lambda i: (i, 0)),
        out_shape=jax.ShapeDtypeStruct((n, D_MODEL), F32),
        compiler_params=_cparams("parallel"),
    )(x, w)


def _rope_tables(seq, batch, dec_batch):
    half = HEAD_DIM // 2
    inv_freq = ROPE_THETA ** (-jnp.arange(half, dtype=F32) / half)
    pos = jnp.concatenate([jnp.tile(jnp.arange(seq), batch), jnp.tile(PAST_LEN + jnp.arange(DEC_SEQ), dec_batch)])
    ang = pos.astype(F32)[:, None] * inv_freq[None, :]
    cos, sin = jnp.cos(ang), jnp.sin(ang)
    reps = LANES // HEAD_DIM
    return jnp.tile(jnp.concatenate([cos, cos], axis=1), (1, reps)), jnp.tile(jnp.concatenate([-sin, sin], axis=1), (1, reps))


def _strided_cache_view(cache, g):
    depth, b, w = cache.shape[:3]
    d = DILATIONS[g]
    assert w == WINDOWS[g] and (d == 1 or d >= DEC_SEQ)
    n_res = 1 if d == 1 else DEC_SEQ
    v = cache.reshape(depth, b, w // d, d, 2 * GROUP_W)[:, :, :, :n_res]
    return v.reshape(depth, b, w // d, n_res * 2 * GROUP_W)


def kernel(x_prompt, x_sample, cache_kv_w128, cache_kv_w512, cache_kv_w2048, state_ssm, state_conv, p_prompt, p_sample, norm1_w, w_in, conv_w, conv_b, dt_bias, a_log, d_skip, ssm_norm_w, w_attn_out, w_ssm_out, w_o, norm2_w, w_up, w_down, w_ple_proj, w_ple_gate, final_norm_w):
    batch, seq, _ = x_prompt.shape
    dec_batch, dec_seq, _ = x_sample.shape
    depth = w_in.shape[0]
    assert dec_seq == DEC_SEQ and seq % (DILATIONS[-1] * KEYS_PER_QUERY_BLOCK) == 0 and seq >= WINDOWS[-1]
    n_prompt = batch * seq
    n_sample = dec_batch * dec_seq
    caches = (cache_kv_w128, cache_kv_w512, cache_kv_w2048)

    c0 = 0
    cols = {}
    for name, width in (("q", 768), ("k", 768), ("v", 768), ("z", SSM_INNER), ("xbc", XBC_DIM), ("dt", SSM_HEADS),
                        ("ga", D_MODEL), ("gb", D_MODEL)):
        cols[name] = (c0, c0 + width)
        c0 += width
    seg = lambda name: w_in[:, :, cols[name][0]:cols[name][1]]
    w_qk = w_in[:, :, :QK_W].astype(BF16)
    w_rest = jnp.concatenate(
        [seg("xbc"), seg("v"), jnp.pad(seg("dt"), ((0, 0), (0, 0), (0, DT_PAD - SSM_HEADS))), seg("z"), seg("ga"), seg("gb")],
        axis=-1).astype(BF16)
    wao, wso, wo = w_attn_out.astype(BF16), w_ssm_out.astype(BF16), w_o.astype(BF16)
    wup, wdn, wpg, wpp = w_up.astype(BF16), w_down.astype(BF16), w_ple_gate.astype(BF16), w_ple_proj.astype(BF16)
    n1 = norm1_w[:, None, :]
    n2 = norm2_w[:, None, :]
    pad_heads = lambda v: jnp.pad(v, ((0, 0), (0, LANES - SSM_HEADS)))[:, None, :]
    ssd_params = (conv_w, conv_b[:, None, :], pad_heads(dt_bias), pad_heads(a_log),
                  jnp.repeat(d_skip, SSM_HEAD_DIM, axis=-1)[:, None, :], ssm_norm_w[:, None, :])
    cos, sin = _rope_tables(seq, batch, dec_batch)
    cache_views = [_strided_cache_view(c, g) for g, c in enumerate(caches)]
    conv_carry = jnp.pad(state_conv, ((0, 0), (0, 0), (0, DEC_SEQ - (CONV_WIDTH - 1)), (0, 0))).reshape(depth, n_sample, XBC_DIM)
    h0_all = state_ssm.reshape(depth, dec_batch, SSM_INNER, SSM_STATE)

    x = jnp.concatenate([x_prompt.reshape(n_prompt, D_MODEL), x_sample.reshape(n_sample, D_MODEL)], axis=0)
    ple = jnp.concatenate([p_prompt.reshape(depth, n_prompt, PLE_DIM), p_sample.reshape(depth, n_sample, PLE_DIM)], axis=1)

    tm_proj = next(t for t in TM_PROJ_CANDIDATES if (n_prompt + n_sample) % t == 0)
    kv_p = [[] for _ in range(N_GROUPS)]
    kv_new = [[] for _ in range(N_GROUPS)]
    ssm_p, ssm_s, conv_p, conv_s = [], [], [], []
    for i in range(depth):
        qk = _inproj(x, n1, w_qk, i, cos, sin, tm=tm_proj, tn=QK_W // 2)
        rest = _inproj(x, n1, w_rest, i, tm=tm_proj, tn=1024)

        o_p, l_p, o_s, l_s = [], [], [], []
        for g in range(N_GROUPS):
            o, l = _attn_prompt(qk, rest, g, batch=batch, seq=seq)
            o_p.append(o)
            l_p.append(l)
            o, l = _attn_sample(qk, rest, cache_views[g], g, i, n_prompt=n_prompt, dec_batch=dec_batch)
            o_s.append(o)
            l_s.append(l)

        y_p, h_p = _ssd_prompt(rest, ssd_params, i, batch=batch, seq=seq)
        y_s, h_s = _ssd_sample(rest, conv_carry, h0_all, ssd_params, i, n_prompt=n_prompt, dec_batch=dec_batch)

        x = _merge(x, o_p, o_s, l_p, l_s, y_p, y_s, rest, wao, wso, wo, i, tm=TM_MERGE)
        x = _mlp(x, ple, n2, wup, wdn, wpg, wpp, i, tm=TM_MLP)

        for g in range(N_GROUPS):
            w = WINDOWS[g]
            ks = slice(N_GROUPS * GROUP_W + g * GROUP_W, N_GROUPS * GROUP_W + (g + 1) * GROUP_W)
            vs = slice(REST_V + g * GROUP_W, REST_V + (g + 1) * GROUP_W)
            k_tail = jnp.stack([qk[b * seq + seq - w:(b + 1) * seq, ks] for b in range(batch)])
            v_tail = jnp.stack([rest[b * seq + seq - w:(b + 1) * seq, vs] for b in range(batch)])
            kv_p[g].append(jnp.stack([k_tail, v_tail], axis=2).reshape(batch, w, 2, ATTN_HPG, HEAD_DIM))
            kvn = jnp.stack([qk[n_prompt:, ks], rest[n_prompt:, vs]], axis=1)
            kv_new[g].append(kvn.reshape(dec_batch, dec_seq, 2, ATTN_HPG, HEAD_DIM))
        tail = CONV_WIDTH - 1
        conv_p.append(jnp.stack([rest[(b + 1) * seq - tail:(b + 1) * seq, :XBC_DIM] for b in range(batch)]))
        xbc_s = rest[n_prompt:, :XBC_DIM].reshape(dec_batch, dec_seq, XBC_DIM)
        conv_s.append(jnp.concatenate([state_conv[i], xbc_s], axis=1)[:, -tail:])
        ssm_p.append(h_p.reshape(batch, SSM_HEADS, SSM_HEAD_DIM, SSM_STATE))
        ssm_s.append(h_s.reshape(dec_batch, SSM_HEADS, SSM_HEAD_DIM, SSM_STATE))

    y = _final_norm(x, final_norm_w[None], tm=TM_MERGE)
    y_prompt = y[:n_prompt].reshape(batch, seq, D_MODEL)
    y_sample = y[n_prompt:].reshape(dec_batch, dec_seq, D_MODEL)
    kv_s = [jnp.concatenate([caches[g][:, :, dec_seq:], jnp.stack(kv_new[g])], axis=2) for g in range(N_GROUPS)]
    return (y_prompt, y_sample,
            jnp.stack(kv_p[0]), jnp.stack(kv_p[1]), jnp.stack(kv_p[2]), jnp.stack(ssm_p), jnp.stack(conv_p),
            kv_s[0], kv_s[1], kv_s[2], jnp.stack(ssm_s), jnp.stack(conv_s))
```
